```python
import math, functools
import jax
import jax.numpy as jnp
from jax import lax
import numpy as np

D_MODEL = 2048
BATCH = 2
SEQ = 8192
DEPTH = 1
DEC_BATCH = 32
DEC_SEQ = 4
PAST_LEN = 16384
PAGE_SIZE = 128

HEAD_DIM = 128
A_HEADS = D_MODEL // (2 * HEAD_DIM)
A_KV_HEADS = 2
IDX_HEADS = 16
IDX_DIM = 64
TOPK_MAX = 256
Q_BLOCK = 128
ROPE_THETA = 10000.0
B_HEADS = D_MODEL // (2 * HEAD_DIM)
B_KEY_DIM = 128
B_VAL_DIM = 128
CONV_WIDTH = 4
DELTA_CHUNK = 64
A_WIDTH = A_HEADS * HEAD_DIM
B_WIDTH = B_HEADS * B_VAL_DIM
CONV_DIM = 2 * B_HEADS * B_KEY_DIM + B_HEADS * B_VAL_DIM
N_GROUPS = 4
EXPERTS_PER_GROUP = 4
N_EXPERTS = N_GROUPS * EXPERTS_PER_GROUP
TOP_K_FINE = 2
D_EXPERT = 512
DEEPNORM_ALPHA = (2 * DEPTH) ** 0.25
DEEPNORM_BETA = (8 * DEPTH) ** -0.25
LN_EPS = 1e-5
RMS_EPS = 1e-6
L2_EPS = 1e-6
SPLIT_SIZES = (A_HEADS * HEAD_DIM, A_KV_HEADS * HEAD_DIM, A_KV_HEADS * HEAD_DIM,
               IDX_HEADS * IDX_DIM, IDX_DIM, IDX_HEADS,
               CONV_DIM, B_HEADS * B_VAL_DIM, B_HEADS, B_HEADS)
SPLIT_POINTS = tuple(int(v) for v in np.cumsum(SPLIT_SIZES)[:-1])
IN_WIDTH = int(sum(SPLIT_SIZES))

kernel_name = 'hymba_dsa_gdn_hmoe_deepnorm_adaln_step'


def layer_norm(x, g=None, b=None):
    xf = x.astype(jnp.float32)
    mu = jnp.mean(xf, -1, keepdims=True)
    xc = xf - mu
    y = xc * lax.rsqrt(jnp.mean(xc * xc, -1, keepdims=True) + LN_EPS)
    if g is not None:
        y = y * g.astype(jnp.float32) + b.astype(jnp.float32)
    return y.astype(x.dtype)


def ada_ln(x, shift, scale):
    return layer_norm(x) * (1.0 + scale) + shift


def l2norm(x):
    return x * lax.rsqrt(jnp.sum(x * x, -1, keepdims=True) + L2_EPS)


def rope(x, pos):
    half = x.shape[-1] // 2
    inv_freq = ROPE_THETA ** (-jnp.arange(half, dtype=jnp.float32) / half)
    ang = pos.astype(jnp.float32)[:, None] * inv_freq[None, :]
    cos = jnp.cos(ang)[None, :, None, :]
    sin = jnp.sin(ang)[None, :, None, :]
    xf = x.astype(jnp.float32)
    x1, x2 = xf[..., :half], xf[..., half:]
    return jnp.concatenate([x1 * cos - x2 * sin, x2 * cos + x1 * sin], -1).astype(x.dtype)


def take_rows(rows, idx):
    return jax.vmap(lambda r, i: r[i])(rows, idx)


def indexer_scores(qi, wi, kidx):
    s = jnp.einsum('bthd,bsd->bths', qi, kidx)
    return jnp.einsum('bths,bth->bts', jax.nn.relu(s), wi)


def select_keys(scores, qpos, kpos, topk):
    s = jnp.where(kpos[None, None, :] <= qpos[None, :, None], scores.astype(jnp.float32), -jnp.inf)
    _, idx = lax.top_k(s, topk)
    return idx


def attend_selected(q, k_sel, v_sel, valid):
    B, T, H, DH = q.shape
    kvh = k_sel.shape[3]
    qg = q.reshape(B, T, kvh, H // kvh, DH)
    s = jnp.einsum('btkgd,btnkd->btkgn', qg, k_sel).astype(jnp.float32) * (DH ** -0.5)
    s = jnp.where(valid[:, :, None, None, :], s, -jnp.inf)
    p = jax.nn.softmax(s, -1).astype(v_sel.dtype)
    o = jnp.einsum('btkgn,btnkd->btkgd', p, v_sel)
    return o.reshape(B, T, H, DH)


def dsa_prompt(q, k, v, qi, ki, wi, pos):
    B, T, H, DH = q.shape
    topk = min(TOPK_MAX, T // 4)
    nb = T // Q_BLOCK

    def to_blocks(a):
        return a.reshape((B, nb, Q_BLOCK) + a.shape[2:]).swapaxes(0, 1)

    def one_block(args):
        q_b, qi_b, wi_b, pos_b = args
        idx = select_keys(indexer_scores(qi_b, wi_b, ki), pos_b, pos, topk)
        return attend_selected(q_b, take_rows(k, idx), take_rows(v, idx), idx <= pos_b[None, :, None])

    o = lax.map(one_block, (to_blocks(q), to_blocks(qi), to_blocks(wi), pos.reshape(nb, Q_BLOCK)))
    return o.swapaxes(0, 1).reshape(B, T, H, DH)


def dsa_sample(q, k, v, qi, ki, wi, pos, cache_k, cache_v, cache_kidx, page_table):
    B, T = q.shape[:2]
    ps = cache_k.shape[1]
    past_len = page_table.shape[1] * ps
    n_keys = past_len + T
    topk = min(TOPK_MAX, n_keys // 4)
    kidx_past = cache_kidx[page_table].reshape(B, past_len, IDX_DIM).astype(ki.dtype)
    kidx_all = jnp.concatenate([kidx_past, ki], axis=1)
    idx = select_keys(indexer_scores(qi, wi, kidx_all), pos, jnp.arange(n_keys, dtype=jnp.int32), topk)
    in_past = idx < past_len
    pidx = jnp.minimum(idx, past_len - 1)
    phys = jnp.take_along_axis(page_table, (pidx // ps).reshape(B, -1), axis=1).reshape(idx.shape)
    off = pidx % ps
    nidx = jnp.clip(idx - past_len, 0, T - 1)
    k_sel = jnp.where(in_past[..., None, None], cache_k[phys, off].astype(k.dtype), take_rows(k, nidx))
    v_sel = jnp.where(in_past[..., None, None], cache_v[phys, off].astype(v.dtype), take_rows(v, nidx))
    return attend_selected(q, k_sel, v_sel, idx <= pos[None, :, None])


def gated_delta_rule(q, k, v, log_g, beta, s0):
    B, T, H, DK = q.shape
    DV = v.shape[-1]
    C = DELTA_CHUNK
    pad = (-T) % C

    def prep(a):
        a = jnp.pad(a, [(0, 0), (0, pad)] + [(0, 0)] * (a.ndim - 2))
        n = a.shape[1] // C
        a = a.reshape((B, n, C) + a.shape[2:])
        return jnp.moveaxis(jnp.moveaxis(a, 3, 2), 1, 0)

    qc, kc, vc, gc, bc = prep(q), prep(k), prep(v), prep(log_g), prep(beta)
    g = jnp.cumsum(gc, axis=-1)
    causal = jnp.tril(jnp.ones((C, C), bool))
    strict = jnp.tril(jnp.ones((C, C), bool), -1)
    decay = jnp.exp(jnp.where(causal, g[..., :, None] - g[..., None, :], -jnp.inf))
    kb = kc * bc[..., None]
    a_mat = jnp.where(strict, jnp.einsum('nbhcd,nbhsd->nbhcs', kb, kc) * decay, 0.0)
    t_mat = a_mat + jnp.eye(C, dtype=jnp.float32)
    u = lax.linalg.triangular_solve(t_mat, vc * bc[..., None], left_side=True, lower=True, unit_diagonal=True)
    w = lax.linalg.triangular_solve(t_mat, kb * jnp.exp(g)[..., None], left_side=True, lower=True, unit_diagonal=True)
    qk = jnp.einsum('nbhcd,nbhsd->nbhcs', qc, kc) * decay

    def step(S, xs):
        q_i, k_i, u_i, w_i, g_i, qk_i = xs
        v_new = u_i - jnp.einsum('bhcd,bhde->bhce', w_i, S)
        o = (jnp.einsum('bhcd,bhde->bhce', q_i * jnp.exp(g_i)[..., None], S)
             + jnp.einsum('bhcs,bhse->bhce', qk_i, v_new))
        g_last = g_i[..., -1:]
        S = (S * jnp.exp(g_last)[..., None]
             + jnp.einsum('bhcd,bhce->bhde', k_i * jnp.exp(g_last - g_i)[..., None], v_new))
        return S, o

    S, o = lax.scan(step, s0, (qc, kc, u, w, g, qk))
    n = o.shape[0]
    o = jnp.moveaxis(jnp.moveaxis(o, 0, 1), 2, 3).reshape(B, n * C, H, DV)[:, :T]
    return o, S


def hier_moe(u, w_grp, b_grp, w_erouter, b_erouter, w_gate, w_up, w_down):
    shp = u.shape
    t = u.reshape(-1, shp[-1])
    grp_logits = (t @ w_grp).astype(jnp.float32) + b_grp.astype(jnp.float32)
    p_grp = jax.nn.softmax(grp_logits, -1)
    g_idx = jnp.argmax(grp_logits, -1)
    g_oh = jax.nn.one_hot(g_idx, N_GROUPS, dtype=jnp.float32)
    p_top = jnp.sum(p_grp * g_oh, -1)
    e_logits = ((t @ w_erouter).astype(jnp.float32) + b_erouter.astype(jnp.float32)).reshape(-1, N_GROUPS, EXPERTS_PER_GROUP)
    e_sel = jnp.einsum('ngk,ng->nk', e_logits, g_oh)
    top_v, top_i = lax.top_k(e_sel, TOP_K_FINE)
    fine = jax.nn.softmax(top_v, -1) * p_top[:, None]
    expert_id = g_idx[:, None] * EXPERTS_PER_GROUP + top_i
    combine = jnp.einsum('nk,nke->ne', fine, jax.nn.one_hot(expert_id, N_EXPERTS, dtype=jnp.float32))
    h = jnp.einsum('nd,edf->nef', t, w_gate)
    up = jnp.einsum('nd,edf->nef', t, w_up)
    act = jax.nn.silu(h) * up * combine[..., None].astype(t.dtype)
    y = jnp.einsum('nef,efd->nd', act, w_down)
    return y.reshape(shp)


def decoder_layer(x, c, pos, attend, conv_buf, ssm_state, w_mod, b_mod, w_in, w_conv, a_log, dt_bias,
                  w_onorm, w_out, ln1_g, ln1_b, w_grp, b_grp, w_erouter, b_erouter, w_gate, w_up, w_down,
                  ln2_g, ln2_b):
    B, T, _ = x.shape
    mod = jax.nn.silu(c) @ w_mod + b_mod
    sh1, sc1, gt1, sh2, sc2, gt2 = [m[:, None, :] for m in jnp.split(mod, 6, axis=-1)]
    u = ada_ln(x, sh1, sc1)
    proj = u @ w_in
    qa, ka, va, qi, ki, wi, qkv_b, z_b, b_raw, a_raw = jnp.split(proj, SPLIT_POINTS, axis=-1)
    qa = rope(qa.reshape(B, T, A_HEADS, HEAD_DIM), pos)
    ka = rope(ka.reshape(B, T, A_KV_HEADS, HEAD_DIM), pos)
    va = va.reshape(B, T, A_KV_HEADS, HEAD_DIM)
    qi = rope(qi.reshape(B, T, IDX_HEADS, IDX_DIM), pos)
    ki = rope(ki[:, :, None, :], pos)[:, :, 0]
    wi = wi * ((IDX_HEADS * IDX_DIM) ** -0.5)
    o_a = attend(qa, ka, va, qi, ki, wi, pos)
    xpad = jnp.concatenate([conv_buf.astype(qkv_b.dtype), qkv_b], axis=1)
    conv = xpad[:, 0:T] * w_conv[0]
    for j in range(1, CONV_WIDTH):
        conv = conv + xpad[:, j:j + T] * w_conv[j]
    conv_new = xpad[:, T:]
    conv = jax.nn.silu(conv).astype(jnp.float32)
    qb, kb, vb = jnp.split(conv, [B_HEADS * B_KEY_DIM, 2 * B_HEADS * B_KEY_DIM], axis=-1)
    qb = l2norm(qb.reshape(B, T, B_HEADS, B_KEY_DIM)) * (B_KEY_DIM ** -0.5)
    kb = l2norm(kb.reshape(B, T, B_HEADS, B_KEY_DIM))
    vb = vb.reshape(B, T, B_HEADS, B_VAL_DIM)
    beta = jax.nn.sigmoid(b_raw.astype(jnp.float32))
    log_g = -jnp.exp(a_log.astype(jnp.float32)) * jax.nn.softplus(a_raw.astype(jnp.float32) + dt_bias.astype(jnp.float32))
    o_b, ssm_new = gated_delta_rule(qb, kb, vb, log_g, beta, ssm_state.astype(jnp.float32))
    o_b = (o_b * lax.rsqrt(jnp.mean(o_b * o_b, -1, keepdims=True) + RMS_EPS) * w_onorm.astype(jnp.float32)
           * jax.nn.silu(z_b.reshape(B, T, B_HEADS, B_VAL_DIM).astype(jnp.float32)))
    mix = jnp.concatenate([o_a.reshape(B, T, A_WIDTH), o_b.reshape(B, T, B_WIDTH).astype(x.dtype)], -1) @ w_out
    x = layer_norm(DEEPNORM_ALPHA * x + gt1 * mix, ln1_g, ln1_b)
    u2 = ada_ln(x, sh2, sc2)
    ffn = hier_moe(u2, w_grp, b_grp, w_erouter, b_erouter, w_gate, w_up, w_down)
    x = layer_norm(DEEPNORM_ALPHA * x + gt2 * ffn, ln2_g, ln2_b)
    return x, (ka, va, ki, conv_new, ssm_new)


def setup_inputs(seed: int = 0) -> dict:
    key = jax.random.key(seed)
    keys = jax.random.split(key, 32)
    kit = iter(range(32))

    def nrm(shape, scale=1.0):
        return jax.random.normal(keys[next(kit)], shape, jnp.float32) * scale

    def unif(shape, lo, hi):
        return jax.random.uniform(keys[next(kit)], shape, jnp.float32, lo, hi)

    n_pages = PAST_LEN // PAGE_SIZE
    n_used = DEC_BATCH * n_pages
    n_pool = n_used + n_used // 4
    d_in = D_MODEL ** -0.5
    x_prompt = nrm((BATCH, SEQ, D_MODEL))
    x_sample = nrm((DEC_BATCH, DEC_SEQ, D_MODEL))
    cache_k = nrm((DEPTH, n_pool, PAGE_SIZE, A_KV_HEADS, HEAD_DIM))
    cache_v = nrm((DEPTH, n_pool, PAGE_SIZE, A_KV_HEADS, HEAD_DIM))
    cache_kidx = nrm((DEPTH, n_pool, PAGE_SIZE, IDX_DIM))
    state_conv = nrm((DEPTH, DEC_BATCH, CONV_WIDTH - 1, CONV_DIM))
    state_ssm = nrm((DEPTH, DEC_BATCH, B_HEADS, B_KEY_DIM, B_VAL_DIM), 0.1)
    page_table = jax.random.permutation(keys[next(kit)], n_pool)[:n_used].reshape(DEC_BATCH, n_pages).astype(jnp.int32)
    c_prompt = nrm((BATCH, D_MODEL))
    c_sample = nrm((DEC_BATCH, D_MODEL))
    w_mod = nrm((DEPTH, D_MODEL, 6 * D_MODEL), d_in)
    b_mod = nrm((DEPTH, 6 * D_MODEL), 0.02)
    w_in = nrm((DEPTH, D_MODEL, IN_WIDTH), d_in)
    w_conv = nrm((DEPTH, CONV_WIDTH, CONV_DIM), CONV_WIDTH ** -0.5)
    a_log = jnp.log(unif((DEPTH, B_HEADS), 1.0, 16.0))
    dt = jnp.exp(unif((DEPTH, B_HEADS), math.log(1e-3), math.log(1e-1)))
    dt_bias = dt + jnp.log(-jnp.expm1(-dt))
    w_onorm = 1.0 + nrm((DEPTH, B_VAL_DIM), 0.02)
    w_out = nrm((DEPTH, D_MODEL, D_MODEL), d_in * DEEPNORM_BETA)
    ln1_g = 1.0 + nrm((DEPTH, D_MODEL), 0.02)
    ln1_b = nrm((DEPTH, D_MODEL), 0.02)
    w_grp = nrm((DEPTH, D_MODEL, N_GROUPS), d_in)
    b_grp = nrm((DEPTH, N_GROUPS), 0.01)
    w_erouter = nrm((DEPTH, D_MODEL, N_EXPERTS), d_in)
    b_erouter = nrm((DEPTH, N_EXPERTS), 0.01)
    w_gate = nrm((DEPTH, N_EXPERTS, D_MODEL, D_EXPERT), d_in)
    w_up = nrm((DEPTH, N_EXPERTS, D_MODEL, D_EXPERT), d_in)
    w_down = nrm((DEPTH, N_EXPERTS, D_EXPERT, D_MODEL), (D_EXPERT ** -0.5) * DEEPNORM_BETA)
    ln2_g = 1.0 + nrm((DEPTH, D_MODEL), 0.02)
    ln2_b = nrm((DEPTH, D_MODEL), 0.02)
    return {'x_prompt': x_prompt, 'x_sample': x_sample, 'cache_k': cache_k, 'cache_v': cache_v,
            'cache_kidx': cache_kidx, 'state_conv': state_conv, 'state_ssm': state_ssm,
            'page_table': page_table, 'c_prompt': c_prompt, 'c_sample': c_sample,
            'w_mod': w_mod, 'b_mod': b_mod, 'w_in': w_in, 'w_conv': w_conv, 'a_log': a_log,
            'dt_bias': dt_bias, 'w_onorm': w_onorm, 'w_out': w_out, 'ln1_g': ln1_g, 'ln1_b': ln1_b,
            'w_grp': w_grp, 'b_grp': b_grp, 'w_erouter': w_erouter, 'b_erouter': b_erouter,
            'w_gate': w_gate, 'w_up': w_up, 'w_down': w_down, 'ln2_g': ln2_g, 'ln2_b': ln2_b}


def reference(x_prompt, x_sample, cache_k, cache_v, cache_kidx, state_conv, state_ssm, page_table,
              c_prompt, c_sample, w_mod, b_mod, w_in, w_conv, a_log, dt_bias, w_onorm, w_out,
              ln1_g, ln1_b, w_grp, b_grp, w_erouter, b_erouter, w_gate, w_up, w_down, ln2_g, ln2_b):
    batch, seq = x_prompt.shape[:2]
    dec_seq = x_sample.shape[1]
    past_len = page_table.shape[1] * cache_k.shape[2]
    pos_p = jnp.arange(seq, dtype=jnp.int32)
    pos_s = past_len + jnp.arange(dec_seq, dtype=jnp.int32)
    conv0 = jnp.zeros((batch, CONV_WIDTH - 1, CONV_DIM), x_prompt.dtype)
    ssm0 = jnp.zeros((batch, B_HEADS, B_KEY_DIM, B_VAL_DIM), jnp.float32)
    h_p, h_s = x_prompt, x_sample
    st_p, st_s = [], []
    for l in range(DEPTH):
        layer_w = (w_mod[l], b_mod[l], w_in[l], w_conv[l], a_log[l], dt_bias[l], w_onorm[l], w_out[l],
                   ln1_g[l], ln1_b[l], w_grp[l], b_grp[l], w_erouter[l], b_erouter[l],
                   w_gate[l], w_up[l], w_down[l], ln2_g[l], ln2_b[l])
        attend_s = functools.partial(dsa_sample, cache_k=cache_k[l], cache_v=cache_v[l],
                                     cache_kidx=cache_kidx[l], page_table=page_table)
        h_p, s_p = decoder_layer(h_p, c_prompt, pos_p, dsa_prompt, conv0, ssm0, *layer_w)
        h_s, s_s = decoder_layer(h_s, c_sample, pos_s, attend_s, state_conv[l], state_ssm[l], *layer_w)
        st_p.append(s_p)
        st_s.append(s_s)
    k_p, v_p, kidx_p, conv_p, ssm_p = [jnp.stack(a, 0) for a in zip(*st_p)]
    k_s, v_s, kidx_s, conv_s, ssm_s = [jnp.stack(a, 0) for a in zip(*st_s)]
    return (h_p, h_s, k_p, v_p, kidx_p, conv_p, ssm_p, k_s, v_s, kidx_s, conv_s, ssm_s)
```

```python
import functools

import numpy as np
import jax
import jax.numpy as jnp
from jax import lax
from jax.experimental import pallas as pl
from jax.experimental.pallas import tpu as pltpu

F32 = jnp.float32
BF16 = jnp.bfloat16
I32 = jnp.int32

HEAD_DIM = 128
A_KV_HEADS = 2
IDX_HEADS = 16
IDX_DIM = 64
TOPK_MAX = 256
ROPE_THETA = 10000.0
B_KEY_DIM = 128
B_VAL_DIM = 128
CONV_WIDTH = 4
DELTA_CHUNK = 64
N_GROUPS = 4
EXPERTS_PER_GROUP = 4
N_EXPERTS = N_GROUPS * EXPERTS_PER_GROUP
LN_EPS = 1e-5
RMS_EPS = 1e-6
L2_EPS = 1e-6

LANES = 128
SUBLANES = 8
VMEM_LIMIT = 56 * 1024 * 1024
INPROJ_TM = 1024
OUTPROJ_TM = 256
MOE_TM = 512

PACK_TN = 256
QA_OFF, QI_OFF, QB_OFF, KB_OFF, VB_OFF, Z_OFF, KA_OFF, VA_OFF, SM_OFF = (
    0, 1024, 2048, 3072, 4096, 5120, 6144, 6400, 6656)
PACK_W = 6912
SM_KI, SM_WI, SM_BETA, SM_DECAY = 0, 64, 80, 88
RT_GRP, RT_EXP = 0, N_GROUPS

NEG_BIG = -1e30
MASK_BIAS = -2e30
MAX_SEARCH_ITERS = 64

NT_DIMS = (((1,), (1,)), ((), ()))
TN_DIMS = (((0,), (0,)), ((), ()))


def _cparams(*sem):
    return pltpu.CompilerParams(dimension_semantics=sem, vmem_limit_bytes=VMEM_LIMIT)


def _ln(x):
    mu = jnp.mean(x, axis=-1, keepdims=True)
    xc = x - mu
    return xc * lax.rsqrt(jnp.mean(xc * xc, axis=-1, keepdims=True) + LN_EPS)


def _sigmoid(x):
    return 1.0 / (1.0 + jnp.exp(-x))


def _silu(x):
    return x * _sigmoid(x)


def _dot(a, b):
    return jnp.dot(a.astype(BF16), b.astype(BF16), preferred_element_type=F32)


def _dot_nt(a, b):
    return lax.dot_general(a.astype(BF16), b.astype(BF16), NT_DIMS, preferred_element_type=F32)


def _split(a):
    hi = a.astype(BF16)
    lo = (a - hi.astype(F32)).astype(BF16)
    return hi, lo


def _dot3(a, b):
    ah, al = _split(a)
    bh, bl = _split(b)
    d = functools.partial(jnp.dot, preferred_element_type=F32)
    return d(ah, bh) + (d(ah, bl) + d(al, bh))


def _kth_threshold(count_ge, mn, mx, n_adm, topk):
    k = float(topk)
    c_top = count_ge(mx)
    all_in = n_adm <= k
    done0 = jnp.where(all_in | (c_top >= k), 1.0, 0.0)
    thr0 = jnp.where(all_in, mn, mx)

    def cond(st):
        return (st[0] < MAX_SEARCH_ITERS) & (st[-1] > 0.0)

    def body(st):
        it, lo, hi, c_lo, c_hi, thr, done_f, _ = st
        done = done_f > 0.5
        mid_b = 0.5 * lo + 0.5 * hi
        frac = (c_lo - k + 0.5) / jnp.maximum(c_lo - c_hi, 1.0)
        mid_i = lo + (hi - lo) * frac
        mid = jnp.where((it & 1) == 0, mid_i, mid_b)
        mid = jnp.where((mid > lo) & (mid < hi), mid, mid_b)
        stuck = jnp.logical_not((mid > lo) & (mid < hi))
        c = count_ge(mid)
        ge = c >= k
        hit = c == k
        thr = jnp.where(done, thr, jnp.where(hit, mid, jnp.where(stuck, lo, thr)))
        done_f = jnp.where(done | hit | stuck, 1.0, 0.0)
        lo, c_lo = jnp.where(ge, mid, lo), jnp.where(ge, c, c_lo)
        hi, c_hi = jnp.where(ge, hi, mid), jnp.where(ge, c_hi, c)
        return it + 1, lo, hi, c_lo, c_hi, thr, done_f, jnp.sum(1.0 - done_f)

    st = (jnp.int32(0), mn, mx, n_adm, c_top, thr0, done0, jnp.sum(1.0 - done0))
    st = lax.while_loop(cond, body, st)
    return jnp.where(st[6] > 0.5, st[5], st[1])


def _mod_kernel(c_ref, w_ref, b_ref, o_ref):
    o_ref[...] = _dot(_silu(c_ref[...]), w_ref[...]) + b_ref[...]


def _modulation(c, w_mod, b_mod):
    m, d = c.shape
    n = w_mod.shape[1]
    tn = 1024
    return pl.pallas_call(
        _mod_kernel,
        out_shape=jax.ShapeDtypeStruct((m, n), F32),
        grid=(n // tn,),
        in_specs=[pl.BlockSpec((m, d), lambda j: (0, 0)),
                  pl.BlockSpec((d, tn), lambda j: (0, j)),
                  pl.BlockSpec((1, tn), lambda j: (0, j))],
        out_specs=pl.BlockSpec((m, tn), lambda j: (0, j)),
        compiler_params=_cparams("arbitrary"),
    )(c, w_mod, b_mod.reshape(1, n))


def _rope_tables(pos, half):
    inv = ROPE_THETA ** (-jnp.arange(half, dtype=F32) / half)
    ang = pos.astype(F32)[:, None] * inv[None, :]
    c, s = jnp.cos(ang), jnp.sin(ang)
    rep = LANES // (2 * half)
    cos = jnp.tile(jnp.concatenate([c, c], -1), (1, rep))
    sin = jnp.tile(jnp.concatenate([-s, s], -1), (1, rep))
    return cos, sin


def _rope128(x, cos, sin):
    return x * cos + pltpu.roll(x, HEAD_DIM // 2, 1) * sin


def _rope64(x, cos, sin):
    lane = lax.broadcasted_iota(I32, x.shape, 1)
    first = (lane % IDX_DIM) < (IDX_DIM // 2)
    rot = jnp.where(first, pltpu.roll(x, LANES - IDX_DIM // 2, 1), pltpu.roll(x, IDX_DIM // 2, 1))
    return x * cos + rot * sin


def _inproj_kernel(x_ref, sh_ref, sc_ref, w_ref, c128_ref, s128_ref, c64_ref, s64_ref,
                   o_ref, u_ref, *, wi_scale):
    n = pl.program_id(1)

    @pl.when(n == 0)
    def _():
        y = _ln(x_ref[...]) * (1.0 + sc_ref[...]) + sh_ref[...]
        u_ref[...] = y.astype(BF16)

    acc = jnp.dot(u_ref[...], w_ref[...], preferred_element_type=F32)
    t_qi, t_qb, t_ka, t_sm = QI_OFF // PACK_TN, QB_OFF // PACK_TN, KA_OFF // PACK_TN, SM_OFF // PACK_TN
    is128 = (n < t_qi) | (n == t_ka)
    is64 = (n >= t_qi) & (n < t_qb)
    is_sm = n == t_sm

    @pl.when(is128)
    def _():
        for j in range(PACK_TN // LANES):
            sl = slice(j * LANES, (j + 1) * LANES)
            o_ref[:, sl] = _rope128(acc[:, sl], c128_ref[...], s128_ref[...])

    @pl.when(is64)
    def _():
        for j in range(PACK_TN // LANES):
            sl = slice(j * LANES, (j + 1) * LANES)
            o_ref[:, sl] = _rope64(acc[:, sl], c64_ref[...], s64_ref[...])

    @pl.when(is_sm)
    def _():
        a = acc[:, :LANES]
        lane = lax.broadcasted_iota(I32, a.shape, 1)
        r = _rope64(a, c64_ref[...], s64_ref[...])
        o_ref[:, :LANES] = jnp.where(lane < SM_WI, r, jnp.where(lane < SM_BETA, a * wi_scale, a))
        o_ref[:, LANES:] = acc[:, LANES:]

    @pl.when(jnp.logical_not(is128 | is64 | is_sm))
    def _():
        o_ref[...] = acc


def _in_projection(x, sh, sc, w_pack, tabs, rows_per_group, tm):
    m, d = x.shape
    c128, s128, c64, s64 = tabs
    nt = PACK_W // PACK_TN
    if sh.ndim == 3:
        tpg = rows_per_group // tm
        mod_spec = pl.BlockSpec((None, 1, d), lambda i, n: (i // tpg, 0, 0))
    else:
        mod_spec = pl.BlockSpec((tm, d), lambda i, n: (i, 0))
    tab_blocks = c128.shape[0] // tm
    tab_spec = pl.BlockSpec((tm, LANES), lambda i, n: (i % tab_blocks, 0))
    wi_scale = float((IDX_HEADS * IDX_DIM) ** -0.5)
    return pl.pallas_call(
        functools.partial(_inproj_kernel, wi_scale=wi_scale),
        out_shape=jax.ShapeDtypeStruct((m, PACK_W), F32),
        grid=(m // tm, nt),
        in_specs=[pl.BlockSpec((tm, d), lambda i, n: (i, 0)), mod_spec, mod_spec,
                  pl.BlockSpec((d, PACK_TN), lambda i, n: (0, n)),
                  tab_spec, tab_spec, tab_spec, tab_spec],
        out_specs=pl.BlockSpec((tm, PACK_TN), lambda i, n: (i, n)),
        scratch_shapes=[pltpu.VMEM((tm, d), BF16)],
        compiler_params=_cparams("arbitrary", "arbitrary"),
    )(x, sh, sc, w_pack, c128, s128, c64, s64)


def _pack_w_in(w_in, d):
    heads = d // (2 * HEAD_DIM)
    conv_dim = 2 * heads * B_KEY_DIM + heads * B_VAL_DIM
    sizes = (heads * HEAD_DIM, A_KV_HEADS * HEAD_DIM, A_KV_HEADS * HEAD_DIM, IDX_HEADS * IDX_DIM,
             IDX_DIM, IDX_HEADS, conv_dim, heads * B_VAL_DIM, heads, heads)
    pts = [int(v) for v in np.cumsum(sizes)[:-1]]
    qa, ka, va, qi, ki, wi, qkv, z, braw, araw = jnp.split(w_in, pts, axis=1)
    small = jnp.concatenate([ki, wi, braw, araw], axis=1)
    small = jnp.pad(small, ((0, 0), (0, PACK_TN - small.shape[1])))
    w = jnp.concatenate([qa, qi, qkv, z, ka, va, small], axis=1)
    assert w.shape[1] == PACK_W
    return w.astype(BF16)


def _dsa_prompt_kernel(q_ref, qi_ref, sm_ref, k_ref, vt_ref, ki_ref, o_ref,
                       qt_ref, qit_ref, w_ref, sc_ref, s_ref, p_ref, m_ref, l_ref, a_ref, acc_ref,
                       *, tq, topk, heads):
    i = pl.program_id(1)
    nkb = i + 1
    tk = tq
    scale = float(HEAD_DIM ** -0.5)
    hpg = heads // A_KV_HEADS
    for h in range(heads):
        qt_ref[h] = jnp.transpose(q_ref[:, h * HEAD_DIM:(h + 1) * HEAD_DIM] * scale).astype(BF16)
    for hp in range(IDX_HEADS * IDX_DIM // LANES):
        t = jnp.transpose(qi_ref[:, hp * LANES:(hp + 1) * LANES])
        for u in range(LANES // IDX_DIM):
            qit_ref[hp * (LANES // IDX_DIM) + u] = t[u * IDX_DIM:(u + 1) * IDX_DIM].astype(BF16)
    w_ref[...] = jnp.transpose(sm_ref[...])[SM_WI:SM_WI + IDX_HEADS]
    qcol = i * tq + lax.broadcasted_iota(I32, (tk, tq), 1)
    krow = lax.broadcasted_iota(I32, (tk, tq), 0)

    def scores(kb, carry):
        mn, mx = carry
        off = pl.multiple_of(kb * tk, tk)
        kblk = ki_ref[pl.ds(off, tk), :]
        acc = jnp.zeros((tk, tq), F32)
        for h in range(IDX_HEADS):
            s = jnp.dot(kblk, qit_ref[h], preferred_element_type=F32)
            acc = acc + jnp.maximum(s, 0.0) * w_ref[h:h + 1, :]
        adm = krow + kb * tk <= qcol
        sc_ref[kb] = jnp.where(adm, acc, -jnp.inf)
        mn = jnp.minimum(mn, jnp.min(jnp.where(adm, acc, jnp.inf), axis=0, keepdims=True))
        mx = jnp.maximum(mx, jnp.max(jnp.where(adm, acc, -jnp.inf), axis=0, keepdims=True))
        return mn, mx

    mn, mx = lax.fori_loop(0, nkb, scores,
                           (jnp.full((1, tq), jnp.inf, F32), jnp.full((1, tq), -jnp.inf, F32)))

    def count_ge(t):
        def body(kb, cnt):
            ind = jnp.where(sc_ref[kb] >= t, 1.0, 0.0)
            return cnt + jnp.sum(ind.reshape(tk // SUBLANES, SUBLANES, tq), axis=0)
        cnt = lax.fori_loop(0, nkb, body, jnp.zeros((SUBLANES, tq), F32))
        return jnp.sum(cnt, axis=0, keepdims=True)

    n_adm = (i * tq + lax.broadcasted_iota(I32, (1, tq), 1) + 1).astype(F32)
    thr = _kth_threshold(count_ge, mn, mx, n_adm, topk)

    m_ref[...] = jnp.full(m_ref.shape, NEG_BIG, F32)
    l_ref[...] = jnp.zeros(l_ref.shape, F32)
    acc_ref[...] = jnp.zeros(acc_ref.shape, F32)

    def attend(kb, c):
        off = pl.multiple_of(kb * tk, tk)
        bias = jnp.where(sc_ref[kb] >= thr, 0.0, MASK_BIAS)
        for h in range(heads):
            g = h // hpg
            kg = k_ref[pl.ds(off, tk), g * HEAD_DIM:(g + 1) * HEAD_DIM]
            s_ref[h] = jnp.dot(kg, qt_ref[h], preferred_element_type=F32) + bias
        for h in range(heads):
            s = s_ref[h]
            m_old = m_ref[h]
            m_new = jnp.maximum(m_old, jnp.max(s, axis=0, keepdims=True))
            p = jnp.exp(s - m_new)
            alpha = jnp.exp(m_old - m_new)
            l_ref[h] = alpha * l_ref[h] + jnp.sum(p, axis=0, keepdims=True)
            p_ref[h] = p.astype(BF16)
            m_ref[h] = m_new
            a_ref[h] = alpha
        for h in range(heads):
            acc_ref[h] = a_ref[h] * acc_ref[h] + jnp.dot(vt_ref[kb, h // hpg], p_ref[h],
                                                         preferred_element_type=F32)
        return c

    lax.fori_loop(0, nkb, attend, 0)
    for h in range(heads):
        o_ref[:, h * HEAD_DIM:(h + 1) * HEAD_DIM] = jnp.transpose(acc_ref[h] / l_ref[h]).astype(BF16)


def _dsa_prompt(packed, k_bf, v_bf, ki_bf, batch, seq, heads):
    tq = 256
    nq = seq // tq
    topk = min(TOPK_MAX, seq // 4)
    aw = heads * HEAD_DIM
    iw = IDX_HEADS * IDX_DIM
    kvw = A_KV_HEADS * HEAD_DIM
    vt = v_bf.reshape(batch, nq, tq, A_KV_HEADS, HEAD_DIM).transpose(0, 1, 3, 4, 2)
    return pl.pallas_call(
        functools.partial(_dsa_prompt_kernel, tq=tq, topk=topk, heads=heads),
        out_shape=jax.ShapeDtypeStruct((batch * seq, aw), BF16),
        grid=(batch, nq),
        in_specs=[pl.BlockSpec((tq, aw), lambda b, i: (b * nq + i, QA_OFF // aw)),
                  pl.BlockSpec((tq, iw), lambda b, i: (b * nq + i, QI_OFF // iw)),
                  pl.BlockSpec((tq, LANES), lambda b, i: (b * nq + i, SM_OFF // LANES)),
                  pl.BlockSpec((seq, kvw), lambda b, i: (b, 0)),
                  pl.BlockSpec((None, nq, A_KV_HEADS, HEAD_DIM, tq), lambda b, i: (b, 0, 0, 0, 0)),
                  pl.BlockSpec((seq, IDX_DIM), lambda b, i: (b, 0))],
        out_specs=pl.BlockSpec((tq, aw), lambda b, i: (b * nq + i, 0)),
        scratch_shapes=[pltpu.VMEM((heads, HEAD_DIM, tq), BF16),
                        pltpu.VMEM((IDX_HEADS, IDX_DIM, tq), BF16),
                        pltpu.VMEM((IDX_HEADS, tq), F32),
                        pltpu.VMEM((nq, tq, tq), F32),
                        pltpu.VMEM((heads, tq, tq), F32),
                        pltpu.VMEM((heads, tq, tq), BF16),
                        pltpu.VMEM((heads, 1, tq), F32),
                        pltpu.VMEM((heads, 1, tq), F32),
                        pltpu.VMEM((heads, 1, tq), F32),
                        pltpu.VMEM((heads, HEAD_DIM, tq), F32)],
        compiler_params=_cparams("arbitrary", "arbitrary"),
    )(packed, packed, packed, k_bf, vt, ki_bf)


def _dsa_sample_kernel(pt_ref, q_ref, qi_ref, sm_ref, kn_ref, vn_ref, *rest,
                       pg, npg, past_len, t_real, topk, heads):
    del pt_ref
    kidx_refs = rest[:pg]
    k_refs = rest[pg:2 * pg]
    v_refs = rest[2 * pg:3 * pg]
    o_ref = rest[3 * pg]
    qi2_ref, w_ref, qt_ref, sc_ref, mm_ref, thr_ref, pad_ref, m_ref, l_ref, acc_ref = rest[3 * pg + 1:]
    j = pl.program_id(1)
    rows = SUBLANES
    ps = LANES
    scale = float(HEAD_DIM ** -0.5)
    hpg = heads // A_KV_HEADS

    def indexer(kt):
        n = kt.shape[1]
        s = jnp.dot(qi2_ref[...], kt.astype(BF16), preferred_element_type=F32)
        x = jnp.maximum(s, 0.0) * jnp.tile(w_ref[...], (1, n // LANES))
        return jnp.sum(x.reshape(IDX_HEADS, rows, n), axis=0)

    def lane_form(x):
        return jnp.transpose(jnp.tile(x, (LANES // rows, 1)))

    def fold_lanes(x, op):
        r = x[:, 0:LANES]
        for t in range(1, x.shape[1] // LANES):
            r = op(r, x[:, t * LANES:(t + 1) * LANES])
        return r

    def padded(block):
        pad_ref[...] = jnp.zeros(pad_ref.shape, F32)
        pad_ref[0:rows, 0:block.shape[1]] = block
        return pad_ref[:, 0:block.shape[1]]

    def softmax_step(s, v_of_group):
        m_old = m_ref[0:1, :]
        m_new = jnp.maximum(m_old, jnp.max(s, axis=0, keepdims=True))
        p = jnp.exp(s - m_new).astype(BF16)
        alpha = jnp.exp(m_old - m_new)
        l_ref[...] = jnp.broadcast_to(alpha * l_ref[0:1, :] + jnp.sum(p.astype(F32), axis=0, keepdims=True),
                                      l_ref.shape)
        m_ref[...] = jnp.broadcast_to(m_new, m_ref.shape)
        for g in range(A_KV_HEADS):
            acc_ref[g] = alpha * acc_ref[g] + lax.dot_general(
                v_of_group(g).astype(BF16), p, TN_DIMS, preferred_element_type=F32)

    @pl.when(j == 0)
    def _():
        for h in range(IDX_HEADS):
            qi2_ref[h * rows:(h + 1) * rows, :] = qi_ref[:, h * IDX_DIM:(h + 1) * IDX_DIM].astype(BF16)
            w_ref[h * rows:(h + 1) * rows, :] = jnp.broadcast_to(
                sm_ref[:, SM_WI + h:SM_WI + h + 1], (rows, LANES))
        for g in range(A_KV_HEADS):
            pad_ref[...] = jnp.zeros(pad_ref.shape, F32)
            for h in range(g * hpg, (g + 1) * hpg):
                pad_ref[h * rows:(h + 1) * rows, :] = q_ref[:, h * HEAD_DIM:(h + 1) * HEAD_DIM] * scale
            qt_ref[g] = jnp.transpose(pad_ref[...]).astype(BF16)
        mm_ref[0] = jnp.full((rows, LANES), jnp.inf, F32)
        mm_ref[1] = jnp.full((rows, LANES), -jnp.inf, F32)
        m_ref[...] = jnp.full(m_ref.shape, NEG_BIG, F32)
        l_ref[...] = jnp.zeros(l_ref.shape, F32)
        acc_ref[...] = jnp.zeros(acc_ref.shape, F32)

    @pl.when(j < npg)
    def _():
        x = indexer(jnp.concatenate([r[...] for r in kidx_refs], axis=1))
        sc_ref[j] = x
        mm_ref[0] = jnp.minimum(mm_ref[0], fold_lanes(x, jnp.minimum))
        mm_ref[1] = jnp.maximum(mm_ref[1], fold_lanes(x, jnp.maximum))

    @pl.when(j == npg - 1)
    def _():
        kt_new = jnp.transpose(padded(sm_ref[...]))[SM_KI:SM_KI + IDX_DIM]
        x = indexer(kt_new)
        qrow = lax.broadcasted_iota(I32, (rows, ps), 0)
        kcol = lax.broadcasted_iota(I32, (rows, ps), 1)
        adm = (kcol <= qrow) & (kcol < t_real)
        sc_ref[npg] = jnp.full(sc_ref.shape[1:], -jnp.inf, F32)
        sc_ref[npg, :, 0:ps] = jnp.where(adm, x, -jnp.inf)
        mn = jnp.min(jnp.minimum(mm_ref[0], jnp.where(adm, x, jnp.inf)), axis=1, keepdims=True)
        mx = jnp.max(jnp.maximum(mm_ref[1], jnp.where(adm, x, -jnp.inf)), axis=1, keepdims=True)

        def count_ge(t):
            def body(b, cnt):
                return cnt + fold_lanes(jnp.where(sc_ref[b] >= t, 1.0, 0.0), jnp.add)
            cnt = lax.fori_loop(0, npg + 1, body, jnp.zeros((rows, LANES), F32))
            return jnp.sum(cnt, axis=1, keepdims=True)

        q1 = lax.broadcasted_iota(I32, (rows, 1), 0)
        n_adm = (past_len + jnp.minimum(q1, t_real - 1) + 1).astype(F32)
        thr = _kth_threshold(count_ge, mn, mx, n_adm, topk)
        thr_ref[...] = lane_form(jnp.broadcast_to(thr, (rows, LANES)))[0:rows]

    @pl.when(j >= npg)
    def _():
        bias = jnp.where(lane_form(sc_ref[j - npg]) >= thr_ref[0:1, :], 0.0, MASK_BIAS)
        s = bias
        for g in range(A_KV_HEADS):
            kg = jnp.concatenate([r[pl.ds(g, ps, stride=A_KV_HEADS), :] for r in k_refs], axis=0)
            s = s + jnp.dot(kg.astype(BF16), qt_ref[g], preferred_element_type=F32)
        softmax_step(s, lambda g: jnp.concatenate([r[pl.ds(g, ps, stride=A_KV_HEADS), :] for r in v_refs], axis=0))

    @pl.when(j == 2 * npg - 1)
    def _():
        s = jnp.where(lane_form(sc_ref[npg, :, 0:ps]) >= thr_ref[0:1, :], 0.0, MASK_BIAS)
        for g in range(A_KV_HEADS):
            kg = padded(kn_ref[:, g * HEAD_DIM:(g + 1) * HEAD_DIM])
            s = s + jnp.dot(kg.astype(BF16), qt_ref[g], preferred_element_type=F32)
        softmax_step(s, lambda g: padded(vn_ref[:, g * HEAD_DIM:(g + 1) * HEAD_DIM]))
        for g in range(A_KV_HEADS):
            out = jnp.transpose(acc_ref[g] / l_ref[0:1, :])
            for h in range(g * hpg, (g + 1) * hpg):
                o_ref[:, h * HEAD_DIM:(h + 1) * HEAD_DIM] = out[h * rows:(h + 1) * rows, :].astype(BF16)


def _dsa_sample(packed3, cache_k, cache_v, cache_kidx, page_table, t_real, heads):
    nb, rows, _ = packed3.shape
    assert rows == SUBLANES and IDX_HEADS * rows == LANES
    ps = cache_k.shape[2]
    assert ps == LANES and cache_k.shape[0] == 1
    n_pages = page_table.shape[1]
    pg = min(16, n_pages)
    npg = n_pages // pg
    assert n_pages % pg == 0
    kvw = A_KV_HEADS * HEAD_DIM
    aw = heads * HEAD_DIM
    iw = IDX_HEADS * IDX_DIM
    past_len = n_pages * ps
    topk = min(TOPK_MAX, (past_len + t_real) // 4)
    pt = page_table.reshape(-1).astype(I32)

    def page1(b, j, pt, i):
        return pt[b * n_pages + jnp.minimum(j, npg - 1) * pg + i]

    def page2(b, j, pt, i):
        return pt[b * n_pages + jnp.maximum(j - npg, 0) * pg + i]

    kidx_t = jnp.swapaxes(cache_kidx, 2, 3)

    def kidx_spec(i):
        return pl.BlockSpec((None, None, IDX_DIM, ps), lambda b, j, pt: (0, page1(b, j, pt, i), 0, 0))

    ck = cache_k.reshape(cache_k.shape[1], ps * A_KV_HEADS, HEAD_DIM)
    cv = cache_v.reshape(cache_v.shape[1], ps * A_KV_HEADS, HEAD_DIM)

    def kv_spec(i):
        return pl.BlockSpec((None, ps * A_KV_HEADS, HEAD_DIM),
                            lambda b, j, pt: (page2(b, j, pt, i), 0, 0))

    in_specs = [pl.BlockSpec((None, rows, aw), lambda b, j, pt: (b, 0, QA_OFF // aw)),
                pl.BlockSpec((None, rows, iw), lambda b, j, pt: (b, 0, QI_OFF // iw)),
                pl.BlockSpec((None, rows, LANES), lambda b, j, pt: (b, 0, SM_OFF // LANES)),
                pl.BlockSpec((None, rows, kvw), lambda b, j, pt: (b, 0, KA_OFF // kvw)),
                pl.BlockSpec((None, rows, kvw), lambda b, j, pt: (b, 0, VA_OFF // kvw))]
    in_specs += [kidx_spec(i) for i in range(pg)]
    in_specs += [kv_spec(i) for i in range(pg)]
    in_specs += [kv_spec(i) for i in range(pg)]
    return pl.pallas_call(
        functools.partial(_dsa_sample_kernel, pg=pg, npg=npg, past_len=past_len, t_real=t_real,
                          topk=topk, heads=heads),
        out_shape=jax.ShapeDtypeStruct((nb, rows, aw), BF16),
        grid_spec=pltpu.PrefetchScalarGridSpec(
            num_scalar_prefetch=1,
            grid=(nb, 2 * npg),
            in_specs=in_specs,
            out_specs=pl.BlockSpec((None, rows, aw), lambda b, j, pt: (b, 0, 0)),
            scratch_shapes=[pltpu.VMEM((IDX_HEADS * rows, IDX_DIM), BF16),
                            pltpu.VMEM((IDX_HEADS * rows, LANES), F32),
                            pltpu.VMEM((A_KV_HEADS, HEAD_DIM, LANES), BF16),
                            pltpu.VMEM((npg + 1, rows, pg * ps), F32),
                            pltpu.VMEM((2, rows, LANES), F32),
                            pltpu.VMEM((rows, LANES), F32),
                            pltpu.VMEM((LANES, LANES), F32),
                            pltpu.VMEM((rows, LANES), F32),
                            pltpu.VMEM((rows, LANES), F32),
                            pltpu.VMEM((A_KV_HEADS, HEAD_DIM, LANES), F32)]),
        compiler_params=_cparams("arbitrary", "arbitrary"),
    )(pt, packed3, packed3, packed3, packed3, packed3,
      *([kidx_t] * pg), *([ck] * pg), *([cv] * pg))


def _unit_lower_inverse(a, c):
    eye = (lax.broadcasted_iota(I32, (c, c), 0) == lax.broadcasted_iota(I32, (c, c), 1)).astype(F32)
    x = eye - a
    p = _dot3(a, a)
    n = 2
    while n < c:
        x = x + _dot3(x, p)
        if 2 * n < c:
            p = _dot3(p, p)
        n *= 2
    return x


def _delta_kernel(qb_ref, kb_ref, vb_ref, z_ref, sm_ref, conv0_ref, wconv_ref, alog_ref, dtb_ref,
                  wnorm_ref, ssm0_ref, o_ref, convo_ref, ssmo_ref, xbuf_ref, s_ref,
                  *, tci, tcc, t_real, heads):
    j = pl.program_id(1)
    c = DELTA_CHUNK
    hw = heads * B_KEY_DIM
    halo = SUBLANES

    @pl.when(j == 0)
    def _():
        xbuf_ref[0:halo, :] = conv0_ref[...]
        s_ref[...] = ssm0_ref[...]

    @pl.when(j > 0)
    def _():
        xbuf_ref[0:halo, :] = xbuf_ref[tcc:tcc + halo, :]

    xbuf_ref[halo:halo + tci, 0:hw] = qb_ref[...]
    xbuf_ref[halo:halo + tci, hw:2 * hw] = kb_ref[...]
    xbuf_ref[halo:halo + tci, 2 * hw:3 * hw] = vb_ref[...]
    if tci < tcc:
        xbuf_ref[halo + tci:halo + tcc, :] = jnp.zeros((tcc - tci, 3 * hw), F32)

    jr, r1 = (t_real - 1) // tcc, t_real - ((t_real - 1) // tcc) * tcc

    @pl.when(j == jr)
    def _():
        convo_ref[...] = jnp.zeros(convo_ref.shape, F32)
        convo_ref[halo - (CONV_WIDTH - 1):halo, :] = xbuf_ref[halo + r1 - (CONV_WIDTH - 1):halo + r1, :]

    conv = xbuf_ref[halo - 3:halo - 3 + tcc, :] * wconv_ref[0:1, :]
    for t in range(1, CONV_WIDTH):
        conv = conv + xbuf_ref[halo - 3 + t:halo - 3 + t + tcc, :] * wconv_ref[t:t + 1, :]
    act = _silu(conv)

    sm = sm_ref[...]
    if tci < tcc:
        sm = jnp.concatenate([sm, jnp.zeros((tcc - tci, LANES), F32)], axis=0)
    lane = lax.broadcasted_iota(I32, (tcc, LANES), 1)
    trow = j * tcc + lax.broadcasted_iota(I32, (tcc, LANES), 0)
    valid = trow < t_real
    beta_t = jnp.where(valid & (lane >= SM_BETA) & (lane < SM_BETA + heads), _sigmoid(sm), 0.0)
    xg = sm + dtb_ref[...]
    softplus = jnp.maximum(xg, 0.0) + jnp.log1p(jnp.exp(-jnp.abs(xg)))
    lg_t = jnp.where(valid & (lane >= SM_DECAY) & (lane < SM_DECAY + heads),
                     -jnp.exp(alog_ref[...]) * softplus, 0.0)
    ri = lax.broadcasted_iota(I32, (tcc, tcc), 0)
    ci = lax.broadcasted_iota(I32, (tcc, tcc), 1)
    tri = ((ci <= ri) & (ci // c == ri // c)).astype(F32)
    g_t = _dot3(tri, lg_t)
    g_tt = jnp.transpose(g_t)
    rmask = valid[:, 0:1]

    cr = lax.broadcasted_iota(I32, (c, c), 0)
    cc = lax.broadcasted_iota(I32, (c, c), 1)
    qscale = float(B_KEY_DIM ** -0.5)
    for ch in range(tcc // c):
        r = slice(ch * c, (ch + 1) * c)
        for h in range(heads):
            qh = act[r, h * B_KEY_DIM:(h + 1) * B_KEY_DIM]
            kh = act[r, hw + h * B_KEY_DIM:hw + (h + 1) * B_KEY_DIM]
            vh = act[r, 2 * hw + h * B_VAL_DIM:2 * hw + (h + 1) * B_VAL_DIM]
            qn = qh * lax.rsqrt(jnp.sum(qh * qh, axis=-1, keepdims=True) + L2_EPS) * qscale
            kn = kh * lax.rsqrt(jnp.sum(kh * kh, axis=-1, keepdims=True) + L2_EPS)
            if t_real % tcc != 0 or tci < tcc:
                kn = jnp.where(rmask[r], kn, 0.0)
                vh = jnp.where(rmask[r], vh, 0.0)
            bcol = beta_t[r, SM_BETA + h:SM_BETA + h + 1]
            gcol = g_t[r, SM_DECAY + h:SM_DECAY + h + 1]
            grow = g_tt[SM_DECAY + h:SM_DECAY + h + 1, r]
            decay = jnp.where(cr >= cc, jnp.exp(jnp.minimum(gcol - grow, 0.0)), 0.0)
            kbeta = kn * bcol
            a_mat = jnp.where(cr > cc, _dot_nt(kbeta, kn) * decay, 0.0)
            tinv = _unit_lower_inverse(a_mat, c)
            egc = jnp.exp(gcol)
            u = _dot3(tinv, vh * bcol)
            w = _dot3(tinv, kbeta * egc)
            qk = _dot_nt(qn, kn) * decay
            s_old = s_ref[h]
            v_new = u - _dot(w, s_old)
            o = _dot(qn * egc, s_old) + _dot(qk, v_new)
            glast = gcol[c - 1:c, :]
            kdec = kn * jnp.exp(glast - gcol)
            s_ref[h] = s_old * jnp.exp(glast) + lax.dot_general(
                kdec.astype(BF16), v_new.astype(BF16), TN_DIMS, preferred_element_type=F32)
            if ch * c < tci:
                ro = slice(ch * c, min((ch + 1) * c, tci))
                nr = ro.stop - ro.start
                zz = z_ref[ro, h * B_VAL_DIM:(h + 1) * B_VAL_DIM]
                on = o[0:nr] * lax.rsqrt(jnp.mean(o[0:nr] * o[0:nr], axis=-1, keepdims=True) + RMS_EPS)
                o_ref[ro, h * B_VAL_DIM:(h + 1) * B_VAL_DIM] = (on * wnorm_ref[...] * _silu(zz)).astype(BF16)

    @pl.when(j == pl.num_programs(1) - 1)
    def _():
        ssmo_ref[...] = s_ref[...]


def _delta_net(packed, conv0, ssm0, w_conv, a_log, dt_bias, w_onorm, batch, t_pad, t_real, tci, tcc, heads):
    hw = heads * B_KEY_DIM
    nt = t_pad // tci
    assert t_pad % tci == 0 and tcc % DELTA_CHUNK == 0 and tci % SUBLANES == 0
    assert (nt == 1 and tci <= tcc) or tci == tcc
    wconv_p = jnp.pad(w_conv, ((0, SUBLANES - CONV_WIDTH), (0, 0)))
    alog_row = jnp.zeros((1, LANES), F32).at[0, SM_DECAY:SM_DECAY + heads].set(a_log)
    dtb_row = jnp.zeros((1, LANES), F32).at[0, SM_DECAY:SM_DECAY + heads].set(dt_bias)
    row = lambda b, j: (b * nt + j)
    return pl.pallas_call(
        functools.partial(_delta_kernel, tci=tci, tcc=tcc, t_real=t_real, heads=heads),
        out_shape=(jax.ShapeDtypeStruct((batch * t_pad, hw), BF16),
                   jax.ShapeDtypeStruct((batch, SUBLANES, 3 * hw), F32),
                   jax.ShapeDtypeStruct((batch, heads, B_KEY_DIM, B_VAL_DIM), F32)),
        grid=(batch, nt),
        in_specs=[pl.BlockSpec((tci, hw), lambda b, j: (row(b, j), QB_OFF // hw)),
                  pl.BlockSpec((tci, hw), lambda b, j: (row(b, j), KB_OFF // hw)),
                  pl.BlockSpec((tci, hw), lambda b, j: (row(b, j), VB_OFF // hw)),
                  pl.BlockSpec((tci, hw), lambda b, j: (row(b, j), Z_OFF // hw)),
                  pl.BlockSpec((tci, LANES), lambda b, j: (row(b, j), SM_OFF // LANES)),
                  pl.BlockSpec((None, SUBLANES, 3 * hw), lambda b, j: (b, 0, 0)),
                  pl.BlockSpec((SUBLANES, 3 * hw), lambda b, j: (0, 0)),
                  pl.BlockSpec((1, LANES), lambda b, j: (0, 0)),
                  pl.BlockSpec((1, LANES), lambda b, j: (0, 0)),
                  pl.BlockSpec((1, B_VAL_DIM), lambda b, j: (0, 0)),
                  pl.BlockSpec((None, heads, B_KEY_DIM, B_VAL_DIM), lambda b, j: (b, 0, 0, 0))],
        out_specs=(pl.BlockSpec((tci, hw), lambda b, j: (row(b, j), 0)),
                   pl.BlockSpec((None, SUBLANES, 3 * hw), lambda b, j: (b, 0, 0)),
                   pl.BlockSpec((None, heads, B_KEY_DIM, B_VAL_DIM), lambda b, j: (b, 0, 0, 0))),
        scratch_shapes=[pltpu.VMEM((tcc + SUBLANES, 3 * hw), F32),
                        pltpu.VMEM((heads, B_KEY_DIM, B_VAL_DIM), F32)],
        compiler_params=_cparams("arbitrary", "arbitrary"),
    )(packed, packed, packed, packed, packed, conv0, wconv_p, alog_row, dtb_row,
      w_onorm.reshape(1, B_VAL_DIM), ssm0)


def _outproj_kernel(oa_ref, ob_ref, x_ref, wa_ref, wb_ref, gt_ref, sh_ref, sc_ref, g_ref, b_ref,
                    wr_ref, br_ref, x1_ref, u2_ref, lg_ref, *, alpha):
    mix = (jnp.dot(oa_ref[...], wa_ref[...], preferred_element_type=F32)
           + jnp.dot(ob_ref[...], wb_ref[...], preferred_element_type=F32))
    x1 = _ln(alpha * x_ref[...] + gt_ref[...] * mix) * g_ref[...] + b_ref[...]
    x1_ref[...] = x1
    u2 = _ln(x1) * (1.0 + sc_ref[...]) + sh_ref[...]
    u2_ref[...] = u2.astype(BF16)
    lg_ref[...] = _dot3(u2, wr_ref[...]) + br_ref[...]


def _out_projection(oa, ob, x, wa, wb, gt, sh, sc, ln_g, ln_b, w_r, b_r, rows_per_group, tm, alpha):
    m, d = x.shape
    hwid = oa.shape[1]
    if gt.ndim == 3:
        tpg = rows_per_group // tm
        mod_spec = pl.BlockSpec((None, 1, d), lambda i: (i // tpg, 0, 0))
    else:
        mod_spec = pl.BlockSpec((tm, d), lambda i: (i, 0))
    full = lambda r, c: pl.BlockSpec((r, c), lambda i: (0, 0))
    return pl.pallas_call(
        functools.partial(_outproj_kernel, alpha=alpha),
        out_shape=(jax.ShapeDtypeStruct((m, d), F32), jax.ShapeDtypeStruct((m, d), BF16),
                   jax.ShapeDtypeStruct((m, LANES), F32)),
        grid=(m // tm,),
        in_specs=[pl.BlockSpec((tm, hwid), lambda i: (i, 0)), pl.BlockSpec((tm, hwid), lambda i: (i, 0)),
                  pl.BlockSpec((tm, d), lambda i: (i, 0)), full(hwid, d), full(hwid, d),
                  mod_spec, mod_spec, mod_spec, full(1, d), full(1, d), full(d, LANES), full(1, LANES)],
        out_specs=(pl.BlockSpec((tm, d), lambda i: (i, 0)), pl.BlockSpec((tm, d), lambda i: (i, 0)),
                   pl.BlockSpec((tm, LANES), lambda i: (i, 0))),
        compiler_params=_cparams("arbitrary"),
    )(oa, ob, x, wa, wb, gt, sh, sc, ln_g.reshape(1, d), ln_b.reshape(1, d), w_r, b_r)


def _combine_weights(lg):
    lane = lax.broadcasted_iota(I32, lg.shape, 1)
    big = jnp.int32(LANES)
    gmask = lane < N_GROUPS
    gl = jnp.where(gmask, lg, -jnp.inf)
    gmax = jnp.max(gl, axis=-1, keepdims=True)
    g_idx = jnp.min(jnp.where(gl == gmax, lane, big), axis=-1, keepdims=True)
    p_top = 1.0 / jnp.sum(jnp.where(gmask, jnp.exp(gl - gmax), 0.0), axis=-1, keepdims=True)
    e_lane = lane - RT_EXP
    in_grp = (e_lane >= 0) & (e_lane < N_EXPERTS) & ((e_lane // EXPERTS_PER_GROUP) == g_idx)
    v = jnp.where(in_grp, lg, -jnp.inf)
    v1 = jnp.max(v, axis=-1, keepdims=True)
    i1 = jnp.min(jnp.where(v == v1, lane, big), axis=-1, keepdims=True)
    vr = jnp.where(lane == i1, -jnp.inf, v)
    v2 = jnp.max(vr, axis=-1, keepdims=True)
    i2 = jnp.min(jnp.where(vr == v2, lane, big), axis=-1, keepdims=True)
    e2 = jnp.exp(v2 - v1)
    den = 1.0 / (1.0 + e2)
    return jnp.where(lane == i1, den * p_top, jnp.where(lane == i2, e2 * den * p_top, 0.0))


def _moe_kernel(u_ref, lg_ref, x1_ref, wg_ref, wu_ref, wd_ref, gt_ref, g_ref, b_ref, o_ref,
                acc_ref, cmb_ref, *, alpha):
    e = pl.program_id(1)

    @pl.when(e == 0)
    def _():
        cmb_ref[...] = _combine_weights(lg_ref[...])
        acc_ref[...] = jnp.zeros(acc_ref.shape, F32)

    u = u_ref[...]
    h = jnp.dot(u, wg_ref[...], preferred_element_type=F32)
    up = jnp.dot(u, wu_ref[...], preferred_element_type=F32)
    lane = lax.broadcasted_iota(I32, cmb_ref.shape, 1)
    c_e = jnp.sum(jnp.where(lane == e + RT_EXP, cmb_ref[...], 0.0), axis=-1, keepdims=True)
    act = _silu(h) * up * c_e
    acc_ref[...] += jnp.dot(act.astype(BF16), wd_ref[...], preferred_element_type=F32)

    @pl.when(e == pl.num_programs(1) - 1)
    def _():
        o_ref[...] = _ln(alpha * x1_ref[...] + gt_ref[...] * acc_ref[...]) * g_ref[...] + b_ref[...]


def _moe(u2, logits, x1, wg, wu, wd, gt, ln_g, ln_b, rows_per_group, tm, alpha):
    m, d = x1.shape
    ne, _, de = wg.shape
    if gt.ndim == 3:
        tpg = rows_per_group // tm
        mod_spec = pl.BlockSpec((None, 1, d), lambda i, e: (i // tpg, 0, 0))
    else:
        mod_spec = pl.BlockSpec((tm, d), lambda i, e: (i, 0))
    return pl.pallas_call(
        functools.partial(_moe_kernel, alpha=alpha),
        out_shape=jax.ShapeDtypeStruct((m, d), F32),
        grid=(m // tm, ne),
        in_specs=[pl.BlockSpec((tm, d), lambda i, e: (i, 0)),
                  pl.BlockSpec((tm, LANES), lambda i, e: (i, 0)),
                  pl.BlockSpec((tm, d), lambda i, e: (i, 0)),
                  pl.BlockSpec((None, d, de), lambda i, e: (e, 0, 0)),
                  pl.BlockSpec((None, d, de), lambda i, e: (e, 0, 0)),
                  pl.BlockSpec((None, de, d), lambda i, e: (e, 0, 0)),
                  mod_spec,
                  pl.BlockSpec((1, d), lambda i, e: (0, 0)),
                  pl.BlockSpec((1, d), lambda i, e: (0, 0))],
        out_specs=pl.BlockSpec((tm, d), lambda i, e: (i, 0)),
        scratch_shapes=[pltpu.VMEM((tm, d), F32), pltpu.VMEM((tm, LANES), F32)],
        compiler_params=_cparams("arbitrary", "arbitrary"),
    )(u2, logits, x1, wg, wu, wd, gt, ln_g.reshape(1, d), ln_b.reshape(1, d))


def _layer(x2d, mod, groups, rows_per_group, t_real, pos, w, attend, conv0, ssm0, tm, delta_tiles):
    m, d = x2d.shape
    heads = d // (2 * HEAD_DIM)
    alpha = float((2 * 1) ** 0.25)
    sh1, sc1, gt1, sh2, sc2, gt2 = jnp.split(mod, 6, axis=-1)
    if rows_per_group % tm == 0:
        expand = lambda a: a[:, None, :]
    else:
        expand = lambda a: jnp.repeat(a, rows_per_group, axis=0)
    sh1, sc1, gt1, sh2, sc2, gt2 = [expand(a) for a in (sh1, sc1, gt1, sh2, sc2, gt2)]
    c128, s128 = _rope_tables(pos, HEAD_DIM // 2)
    c64, s64 = _rope_tables(pos, IDX_DIM // 2)
    tabs = (c128, s128, c64, s64)
    if rows_per_group % tm != 0:
        tabs = tuple(jnp.tile(t, (tm // rows_per_group, 1)) for t in tabs)
    packed = _in_projection(x2d, sh1, sc1, w["w_in"], tabs, rows_per_group, tm)
    o_a = attend(packed)
    tci, tcc = delta_tiles
    o_b, conv_new, ssm_new = _delta_net(packed, conv0, ssm0, w["w_conv"], w["a_log"], w["dt_bias"],
                                        w["w_onorm"], groups, rows_per_group, t_real, tci, tcc, heads)
    x1, u2, logits = _out_projection(o_a, o_b, x2d, w["w_out_a"], w["w_out_b"], gt1, sh2, sc2,
                                     w["ln1_g"], w["ln1_b"], w["w_router"], w["b_router"],
                                     rows_per_group, min(tm, OUTPROJ_TM), alpha)
    x2 = _moe(u2, logits, x1, w["w_gate"], w["w_up"], w["w_down"], gt2, w["ln2_g"], w["ln2_b"],
              rows_per_group, min(tm, MOE_TM), alpha)
    return x2, packed, conv_new, ssm_new


def kernel(x_prompt, x_sample, cache_k, cache_v, cache_kidx, state_conv, state_ssm, page_table,
           c_prompt, c_sample, w_mod, b_mod, w_in, w_conv, a_log, dt_bias, w_onorm, w_out,
           ln1_g, ln1_b, w_grp, b_grp, w_erouter, b_erouter, w_gate, w_up, w_down, ln2_g, ln2_b):
    assert w_mod.shape[0] == 1, "single layer"
    batch, seq, d = x_prompt.shape
    nb, dec_seq, _ = x_sample.shape
    heads = d // (2 * HEAD_DIM)
    hw = heads * B_KEY_DIM
    past_len = page_table.shape[1] * cache_k.shape[2]
    kvw = A_KV_HEADS * HEAD_DIM

    n_rt = N_GROUPS + N_EXPERTS
    w_router = jnp.pad(jnp.concatenate([w_grp[0], w_erouter[0]], axis=1), ((0, 0), (0, LANES - n_rt)))
    b_router = jnp.pad(jnp.concatenate([b_grp[0], b_erouter[0]]), (0, LANES - n_rt)).reshape(1, LANES)
    w = dict(w_in=_pack_w_in(w_in[0], d), w_conv=w_conv[0], a_log=a_log[0], dt_bias=dt_bias[0],
             w_onorm=w_onorm[0], w_out_a=w_out[0, :hw].astype(BF16), w_out_b=w_out[0, hw:].astype(BF16),
             ln1_g=ln1_g[0], ln1_b=ln1_b[0], w_router=w_router, b_router=b_router,
             w_gate=w_gate[0].astype(BF16), w_up=w_up[0].astype(BF16), w_down=w_down[0].astype(BF16),
             ln2_g=ln2_g[0], ln2_b=ln2_b[0])

    n_c = batch + nb
    c_all = jnp.pad(jnp.concatenate([c_prompt, c_sample], axis=0), ((0, (-n_c) % SUBLANES), (0, 0)))
    mod = _modulation(c_all, w_mod[0], b_mod[0])

    def attend_prompt(packed):
        k_bf = packed[:, KA_OFF:KA_OFF + kvw].astype(BF16)
        v_bf = packed[:, VA_OFF:VA_OFF + kvw].astype(BF16)
        ki_bf = packed[:, SM_OFF + SM_KI:SM_OFF + SM_KI + IDX_DIM].astype(BF16)
        return _dsa_prompt(packed, k_bf, v_bf, ki_bf, batch, seq, heads)

    tm_p = min(INPROJ_TM, seq)
    conv0_p = jnp.zeros((batch, SUBLANES, 3 * hw), F32)
    ssm0_p = jnp.zeros((batch, heads, B_KEY_DIM, B_VAL_DIM), F32)
    tc_p = min(128, seq)
    y_p, packed_p, conv_p, ssm_p = _layer(
        x_prompt.reshape(batch * seq, d), mod[:batch], batch, seq, seq, jnp.arange(seq, dtype=I32), w,
        attend_prompt, conv0_p, ssm0_p, tm_p, (tc_p, tc_p))

    rows = SUBLANES
    assert CONV_WIDTH - 1 <= dec_seq <= rows
    x_s = jnp.pad(x_sample, ((0, 0), (0, rows - dec_seq), (0, 0))).reshape(nb * rows, d)
    pos_s = past_len + jnp.arange(rows, dtype=I32)

    def attend_sample(packed):
        o = _dsa_sample(packed.reshape(nb, rows, PACK_W), cache_k, cache_v, cache_kidx,
                        page_table, dec_seq, heads)
        return o.reshape(nb * rows, heads * HEAD_DIM)

    conv0_s = jnp.pad(state_conv[0], ((0, 0), (rows - (CONV_WIDTH - 1), 0), (0, 0)))
    tm_s = min(256, nb * rows)
    y_s, packed_s, conv_s, ssm_s = _layer(
        x_s, mod[batch:batch + nb], nb, rows, dec_seq, pos_s, w, attend_sample, conv0_s, state_ssm[0],
        tm_s, (rows, DELTA_CHUNK))

    def states(packed, groups, t_pad, t):
        p = packed.reshape(groups, t_pad, PACK_W)[:, :t]
        k = p[..., KA_OFF:KA_OFF + kvw].reshape(1, groups, t, A_KV_HEADS, HEAD_DIM)
        v = p[..., VA_OFF:VA_OFF + kvw].reshape(1, groups, t, A_KV_HEADS, HEAD_DIM)
        ki = p[..., SM_OFF + SM_KI:SM_OFF + SM_KI + IDX_DIM][None]
        return k, v, ki

    k_p, v_p, ki_p = states(packed_p, batch, seq, seq)
    k_s, v_s, ki_s = states(packed_s, nb, rows, dec_seq)
    conv_p = conv_p[None, :, rows - (CONV_WIDTH - 1):]
    conv_s = conv_s[None, :, rows - (CONV_WIDTH - 1):]
    y_s = y_s.reshape(nb, rows, d)[:, :dec_seq]
    return (y_p.reshape(batch, seq, d), y_s, k_p, v_p, ki_p, conv_p, ssm_p[None],
            k_s, v_s, ki_s, conv_s, ssm_s[None])
```

```python
import functools

import numpy as np
import jax
import jax.numpy as jnp
from jax import lax
from jax.experimental import pallas as pl
from jax.experimental.pallas import tpu as pltpu

F32 = jnp.float32
BF16 = jnp.bfloat16
I32 = jnp.int32

HEAD_DIM = 128
A_KV_HEADS = 2
IDX_HEADS = 16
IDX_DIM = 64
TOPK_MAX = 256
ROPE_THETA = 10000.0
B_KEY_DIM = 128
B_VAL_DIM = 128
CONV_WIDTH = 4
DELTA_CHUNK = 64
DELTA_GROUP = 4
N_GROUPS = 4
EXPERTS_PER_GROUP = 4
N_EXPERTS = N_GROUPS * EXPERTS_PER_GROUP
LN_EPS = 1e-5
RMS_EPS = 1e-6
L2_EPS = 1e-6

LANES = 128
SUBLANES = 8
VMEM_LIMIT = 56 * 1024 * 1024
INPROJ_TM = 1024
OUTPROJ_TM = 256
MOE_TM = 512
DELTA_TM = 256
DSA_TQ = 256
SAMPLE_PAGES_PER_STEP = 32

PACK_TN = 512
QA_OFF, QI_OFF, QB_OFF, KB_OFF, VB_OFF, Z_OFF, KA_OFF, VA_OFF, SM_OFF = (
    0, 1024, 2048, 3072, 4096, 5120, 6144, 6400, 6656)
PACK_W = 7168
SM_KI, SM_WI, SM_BETA, SM_DECAY = 0, 64, 80, 88
RT_GRP, RT_EXP = 0, N_GROUPS

NEG_BIG = -1e30
MASK_BIAS = -2e30
MAX_SEARCH_ITERS = 64

NT_DIMS = (((1,), (1,)), ((), ()))
TN_DIMS = (((0,), (0,)), ((), ()))


def _cparams(*sem):
    return pltpu.CompilerParams(dimension_semantics=sem, vmem_limit_bytes=VMEM_LIMIT)


def _ln(x):
    mu = jnp.mean(x, axis=-1, keepdims=True)
    xc = x - mu
    return xc * lax.rsqrt(jnp.mean(xc * xc, axis=-1, keepdims=True) + LN_EPS)


def _sigmoid(x):
    return 1.0 / (1.0 + jnp.exp(-x))


def _silu(x):
    return x * _sigmoid(x)


def _dot(a, b):
    return jnp.dot(a.astype(BF16), b.astype(BF16), preferred_element_type=F32)


def _dot_nt(a, b):
    return lax.dot_general(a.astype(BF16), b.astype(BF16), NT_DIMS, preferred_element_type=F32)


def _split(a):
    hi = a.astype(BF16)
    lo = (a - hi.astype(F32)).astype(BF16)
    return hi, lo


def _dot3(a, b):
    ah, al = _split(a)
    bh, bl = _split(b)
    d = functools.partial(jnp.dot, preferred_element_type=F32)
    return d(ah, bh) + (d(ah, bl) + d(al, bh))


def _kth_threshold(count_ge, mn, mx, n_adm, topk):
    k = float(topk)
    all_in = n_adm <= k
    done0 = jnp.where(all_in | (count_ge(mx) >= k), 1.0, 0.0)
    thr0 = jnp.where(all_in, mn, mx)

    def cond(st):
        return (st[0] < MAX_SEARCH_ITERS) & (st[-1] > 0.0)

    def body(st):
        it, lo, hi, thr, done_f, _ = st
        done = done_f > 0.5
        mid = 0.5 * lo + 0.5 * hi
        stuck = jnp.logical_not((mid > lo) & (mid < hi))
        c = count_ge(mid)
        ge = c >= k
        hit = c == k
        thr = jnp.where(done, thr, jnp.where(hit, mid, jnp.where(stuck, lo, thr)))
        done_f = jnp.where(done | hit | stuck, 1.0, 0.0)
        return it + 1, jnp.where(ge, mid, lo), jnp.where(ge, hi, mid), thr, done_f, jnp.sum(1.0 - done_f)

    st = lax.while_loop(cond, body, (jnp.int32(0), mn, mx, thr0, done0, jnp.sum(1.0 - done0)))
    return jnp.where(st[4] > 0.5, st[3], st[1])


def _mod_kernel(c_ref, w_ref, b_ref, o_ref):
    o_ref[...] = _dot(_silu(c_ref[...]), w_ref[...]) + b_ref[...]


def _modulation(c, w_mod, b_mod):
    m, d = c.shape
    n = w_mod.shape[1]
    tn = 1024
    return pl.pallas_call(
        _mod_kernel,
        out_shape=jax.ShapeDtypeStruct((m, n), F32),
        grid=(n // tn,),
        in_specs=[pl.BlockSpec((m, d), lambda j: (0, 0)),
                  pl.BlockSpec((d, tn), lambda j: (0, j)),
                  pl.BlockSpec((1, tn), lambda j: (0, j))],
        out_specs=pl.BlockSpec((m, tn), lambda j: (0, j)),
        compiler_params=_cparams("arbitrary"),
    )(c, w_mod, b_mod.reshape(1, n))


def _rope_tables(pos, half):
    inv = ROPE_THETA ** (-jnp.arange(half, dtype=F32) / half)
    ang = pos.astype(F32)[:, None] * inv[None, :]
    c, s = jnp.cos(ang), jnp.sin(ang)
    rep = LANES // (2 * half)
    cos = jnp.tile(jnp.concatenate([c, c], -1), (1, rep))
    sin = jnp.tile(jnp.concatenate([-s, s], -1), (1, rep))
    return cos, sin


def _rope128(x, cos, sin):
    return x * cos + pltpu.roll(x, HEAD_DIM // 2, 1) * sin


def _rope64(x, cos, sin):
    lane = lax.broadcasted_iota(I32, x.shape, 1)
    first = (lane % IDX_DIM) < (IDX_DIM // 2)
    rot = jnp.where(first, pltpu.roll(x, LANES - IDX_DIM // 2, 1), pltpu.roll(x, IDX_DIM // 2, 1))
    return x * cos + rot * sin


def _inproj_kernel(x_ref, sh_ref, sc_ref, w_ref, c128_ref, s128_ref, c64_ref, s64_ref,
                   o_ref, u_ref, *, wi_scale):
    n = pl.program_id(1)

    @pl.when(n == 0)
    def _():
        y = _ln(x_ref[...]) * (1.0 + sc_ref[...]) + sh_ref[...]
        u_ref[...] = y.astype(BF16)

    acc = jnp.dot(u_ref[...], w_ref[...], preferred_element_type=F32)

    def kind(col):
        if col < QI_OFF or KA_OFF <= col < VA_OFF:
            return "rope128"
        if col < QB_OFF:
            return "rope64"
        return "small" if col == SM_OFF else "plain"

    def store(j, k):
        sl = slice(j * LANES, (j + 1) * LANES)
        a = acc[:, sl]
        if k == "rope128":
            a = _rope128(a, c128_ref[...], s128_ref[...])
        elif k == "rope64":
            a = _rope64(a, c64_ref[...], s64_ref[...])
        elif k == "small":
            lane = lax.broadcasted_iota(I32, a.shape, 1)
            r = _rope64(a, c64_ref[...], s64_ref[...])
            a = jnp.where(lane < SM_WI, r, jnp.where(lane < SM_BETA, a * wi_scale, a))
        o_ref[:, sl] = a

    sub = PACK_TN // LANES
    tile_kinds = [tuple(kind(t * PACK_TN + j * LANES) for j in range(sub)) for t in range(PACK_W // PACK_TN)]
    plain = ("plain",) * sub
    for kinds in sorted(set(tile_kinds) - {plain}):
        tiles = [t for t, k in enumerate(tile_kinds) if k == kinds]
        cond = functools.reduce(jnp.logical_or, [n == t for t in tiles])

        @pl.when(cond)
        def _(kinds=kinds):
            for j, k in enumerate(kinds):
                store(j, k)

    @pl.when(functools.reduce(jnp.logical_and, [n != t for t, k in enumerate(tile_kinds) if k != plain]))
    def _():
        o_ref[...] = acc


def _in_projection(x, sh, sc, w_pack, tabs, rows_per_group, tm):
    m, d = x.shape
    c128, s128, c64, s64 = tabs
    nt = PACK_W // PACK_TN
    if sh.ndim == 3:
        tpg = rows_per_group // tm
        mod_spec = pl.BlockSpec((None, 1, d), lambda i, n: (i // tpg, 0, 0))
    else:
        mod_spec = pl.BlockSpec((tm, d), lambda i, n: (i, 0))
    tab_blocks = c128.shape[0] // tm
    tab_spec = pl.BlockSpec((tm, LANES), lambda i, n: (i % tab_blocks, 0))
    wi_scale = float((IDX_HEADS * IDX_DIM) ** -0.5)
    return pl.pallas_call(
        functools.partial(_inproj_kernel, wi_scale=wi_scale),
        out_shape=jax.ShapeDtypeStruct((m, PACK_W), F32),
        grid=(m // tm, nt),
        in_specs=[pl.BlockSpec((tm, d), lambda i, n: (i, 0)), mod_spec, mod_spec,
                  pl.BlockSpec((d, PACK_TN), lambda i, n: (0, n)),
                  tab_spec, tab_spec, tab_spec, tab_spec],
        out_specs=pl.BlockSpec((tm, PACK_TN), lambda i, n: (i, n)),
        scratch_shapes=[pltpu.VMEM((tm, d), BF16)],
        compiler_params=_cparams("arbitrary", "arbitrary"),
    )(x, sh, sc, w_pack, c128, s128, c64, s64)


def _pack_w_in(w_in, d):
    heads = d // (2 * HEAD_DIM)
    conv_dim = 2 * heads * B_KEY_DIM + heads * B_VAL_DIM
    sizes = (heads * HEAD_DIM, A_KV_HEADS * HEAD_DIM, A_KV_HEADS * HEAD_DIM, IDX_HEADS * IDX_DIM,
             IDX_DIM, IDX_HEADS, conv_dim, heads * B_VAL_DIM, heads, heads)
    pts = [int(v) for v in np.cumsum(sizes)[:-1]]
    qa, ka, va, qi, ki, wi, qkv, z, braw, araw = jnp.split(w_in, pts, axis=1)
    small = jnp.concatenate([ki, wi, braw, araw], axis=1)
    small = jnp.pad(small, ((0, 0), (0, PACK_W - SM_OFF - small.shape[1])))
    w = jnp.concatenate([qa, qi, qkv, z, ka, va, small], axis=1)
    assert w.shape[1] == PACK_W
    return w.astype(BF16)


def _dsa_prompt_kernel(q_ref, qi_ref, sm_ref, k_ref, vt_ref, ki_ref, o_ref,
                       qt_ref, qit_ref, w_ref, sc_ref, s_ref, p_ref, m_ref, l_ref, a_ref, acc_ref,
                       *, tq, topk, heads):
    i = pl.program_id(1)
    nkb = i + 1
    tk = tq
    scale = float(HEAD_DIM ** -0.5)
    hpg = heads // A_KV_HEADS
    for h in range(heads):
        qt_ref[h] = jnp.transpose(q_ref[:, h * HEAD_DIM:(h + 1) * HEAD_DIM] * scale).astype(BF16)
    for hp in range(IDX_HEADS * IDX_DIM // LANES):
        t = jnp.transpose(qi_ref[:, hp * LANES:(hp + 1) * LANES])
        for u in range(LANES // IDX_DIM):
            qit_ref[hp * (LANES // IDX_DIM) + u] = t[u * IDX_DIM:(u + 1) * IDX_DIM].astype(BF16)
    w_ref[...] = jnp.transpose(sm_ref[...])[SM_WI:SM_WI + IDX_HEADS]
    qcol = i * tq + lax.broadcasted_iota(I32, (tk, tq), 1)
    krow = lax.broadcasted_iota(I32, (tk, tq), 0)

    def scores(kb, carry):
        mn, mx = carry
        off = pl.multiple_of(kb * tk, tk)
        kblk = ki_ref[pl.ds(off, tk), :]
        acc = jnp.zeros((tk, tq), F32)
        for h in range(IDX_HEADS):
            s = jnp.dot(kblk, qit_ref[h], preferred_element_type=F32)
            acc = acc + jnp.maximum(s, 0.0) * w_ref[h:h + 1, :]
        adm = krow + kb * tk <= qcol
        sc_ref[kb] = jnp.where(adm, acc, -jnp.inf)
        mn = jnp.minimum(mn, jnp.min(jnp.where(adm, acc, jnp.inf), axis=0, keepdims=True))
        mx = jnp.maximum(mx, jnp.max(jnp.where(adm, acc, -jnp.inf), axis=0, keepdims=True))
        return mn, mx

    mn, mx = lax.fori_loop(0, nkb, scores,
                           (jnp.full((1, tq), jnp.inf, F32), jnp.full((1, tq), -jnp.inf, F32)))

    def count_ge(t):
        def body(kb, cnt):
            ind = jnp.where(sc_ref[kb] >= t, 1.0, 0.0)
            return cnt + jnp.sum(ind.reshape(tk // SUBLANES, SUBLANES, tq), axis=0)
        cnt = lax.fori_loop(0, nkb, body, jnp.zeros((SUBLANES, tq), F32))
        return jnp.sum(cnt, axis=0, keepdims=True)

    n_adm = (i * tq + lax.broadcasted_iota(I32, (1, tq), 1) + 1).astype(F32)
    thr = _kth_threshold(count_ge, mn, mx, n_adm, topk)

    m_ref[...] = jnp.full(m_ref.shape, NEG_BIG, F32)
    l_ref[...] = jnp.zeros(l_ref.shape, F32)
    acc_ref[...] = jnp.zeros(acc_ref.shape, F32)

    def attend(kb, c):
        off = pl.multiple_of(kb * tk, tk)
        bias = jnp.where(sc_ref[kb] >= thr, 0.0, MASK_BIAS)
        for h in range(heads):
            g = h // hpg
            kg = k_ref[pl.ds(off, tk), g * HEAD_DIM:(g + 1) * HEAD_DIM]
            s_ref[h] = jnp.dot(kg, qt_ref[h], preferred_element_type=F32) + bias
        for h in range(heads):
            s = s_ref[h]
            m_old = m_ref[h]
            m_new = jnp.maximum(m_old, jnp.max(s, axis=0, keepdims=True))
            p = jnp.exp(s - m_new)
            alpha = jnp.exp(m_old - m_new)
            l_ref[h] = alpha * l_ref[h] + jnp.sum(p, axis=0, keepdims=True)
            p_ref[h] = p.astype(BF16)
            m_ref[h] = m_new
            a_ref[h] = alpha
        for h in range(heads):
            acc_ref[h] = a_ref[h] * acc_ref[h] + jnp.dot(vt_ref[kb, h // hpg], p_ref[h],
                                                         preferred_element_type=F32)
        return c

    lax.fori_loop(0, nkb, attend, 0)
    for h in range(heads):
        o_ref[:, h * HEAD_DIM:(h + 1) * HEAD_DIM] = jnp.transpose(acc_ref[h] / l_ref[h]).astype(BF16)


def _dsa_prompt(packed, k_bf, v_bf, ki_bf, batch, seq, heads):
    tq = min(DSA_TQ, seq)
    nq = seq // tq
    topk = min(TOPK_MAX, seq // 4)
    aw = heads * HEAD_DIM
    iw = IDX_HEADS * IDX_DIM
    kvw = A_KV_HEADS * HEAD_DIM
    vt = v_bf.reshape(batch, nq, tq, A_KV_HEADS, HEAD_DIM).transpose(0, 1, 3, 4, 2)
    return pl.pallas_call(
        functools.partial(_dsa_prompt_kernel, tq=tq, topk=topk, heads=heads),
        out_shape=jax.ShapeDtypeStruct((batch * seq, aw), BF16),
        grid=(batch, nq),
        in_specs=[pl.BlockSpec((tq, aw), lambda b, i: (b * nq + i, QA_OFF // aw)),
                  pl.BlockSpec((tq, iw), lambda b, i: (b * nq + i, QI_OFF // iw)),
                  pl.BlockSpec((tq, LANES), lambda b, i: (b * nq + i, SM_OFF // LANES)),
                  pl.BlockSpec((seq, kvw), lambda b, i: (b, 0)),
                  pl.BlockSpec((None, nq, A_KV_HEADS, HEAD_DIM, tq), lambda b, i: (b, 0, 0, 0, 0)),
                  pl.BlockSpec((seq, IDX_DIM), lambda b, i: (b, 0))],
        out_specs=pl.BlockSpec((tq, aw), lambda b, i: (b * nq + i, 0)),
        scratch_shapes=[pltpu.VMEM((heads, HEAD_DIM, tq), BF16),
                        pltpu.VMEM((IDX_HEADS, IDX_DIM, tq), BF16),
                        pltpu.VMEM((IDX_HEADS, tq), F32),
                        pltpu.VMEM((nq, tq, tq), F32),
                        pltpu.VMEM((heads, tq, tq), F32),
                        pltpu.VMEM((heads, tq, tq), BF16),
                        pltpu.VMEM((heads, 1, tq), F32),
                        pltpu.VMEM((heads, 1, tq), F32),
                        pltpu.VMEM((heads, 1, tq), F32),
                        pltpu.VMEM((heads, HEAD_DIM, tq), F32)],
        compiler_params=_cparams("arbitrary", "arbitrary"),
    )(packed, packed, packed, k_bf, vt, ki_bf)


def _dsa_sample_kernel(pt_ref, q_ref, qi_ref, sm_ref, kn_ref, vn_ref, *rest,
                       pg, npg, past_len, t_real, topk, heads):
    del pt_ref
    kidx_refs = rest[:pg]
    k_refs = rest[pg:2 * pg]
    v_refs = rest[2 * pg:3 * pg]
    o_ref = rest[3 * pg]
    qi2_ref, w_ref, qt_ref, sc_ref, mm_ref, thr_ref, pad_ref, m_ref, l_ref, acc_ref = rest[3 * pg + 1:]
    j = pl.program_id(1)
    rows = SUBLANES
    ps = LANES
    scale = float(HEAD_DIM ** -0.5)
    hpg = heads // A_KV_HEADS

    def indexer(kt):
        n = kt.shape[1]
        s = jnp.dot(qi2_ref[...], kt.astype(BF16), preferred_element_type=F32)
        x = jnp.maximum(s, 0.0) * jnp.tile(w_ref[...], (1, n // LANES))
        return jnp.sum(x.reshape(IDX_HEADS, rows, n), axis=0)

    def lane_form(x):
        return jnp.transpose(jnp.tile(x, (LANES // rows, 1)))

    def fold_lanes(x, op):
        r = x[:, 0:LANES]
        for t in range(1, x.shape[1] // LANES):
            r = op(r, x[:, t * LANES:(t + 1) * LANES])
        return r

    def padded(block):
        pad_ref[...] = jnp.zeros(pad_ref.shape, F32)
        pad_ref[0:rows, 0:block.shape[1]] = block
        return pad_ref[:, 0:block.shape[1]]

    def softmax_step(s, v_of_group):
        m_old = m_ref[0:1, :]
        m_new = jnp.maximum(m_old, jnp.max(s, axis=0, keepdims=True))
        p = jnp.exp(s - m_new).astype(BF16)
        alpha = jnp.exp(m_old - m_new)
        l_ref[...] = jnp.broadcast_to(alpha * l_ref[0:1, :] + jnp.sum(p.astype(F32), axis=0, keepdims=True),
                                      l_ref.shape)
        m_ref[...] = jnp.broadcast_to(m_new, m_ref.shape)
        for g in range(A_KV_HEADS):
            acc_ref[g] = alpha * acc_ref[g] + lax.dot_general(
                v_of_group(g).astype(BF16), p, TN_DIMS, preferred_element_type=F32)

    @pl.when(j == 0)
    def _():
        for h in range(IDX_HEADS):
            qi2_ref[h * rows:(h + 1) * rows, :] = qi_ref[:, h * IDX_DIM:(h + 1) * IDX_DIM].astype(BF16)
            w_ref[h * rows:(h + 1) * rows, :] = jnp.broadcast_to(
                sm_ref[:, SM_WI + h:SM_WI + h + 1], (rows, LANES))
        for g in range(A_KV_HEADS):
            pad_ref[...] = jnp.zeros(pad_ref.shape, F32)
            for h in range(g * hpg, (g + 1) * hpg):
                pad_ref[h * rows:(h + 1) * rows, :] = q_ref[:, h * HEAD_DIM:(h + 1) * HEAD_DIM] * scale
            qt_ref[g] = jnp.transpose(pad_ref[...]).astype(BF16)
        mm_ref[0] = jnp.full((rows, LANES), jnp.inf, F32)
        mm_ref[1] = jnp.full((rows, LANES), -jnp.inf, F32)
        m_ref[...] = jnp.full(m_ref.shape, NEG_BIG, F32)
        l_ref[...] = jnp.zeros(l_ref.shape, F32)
        acc_ref[...] = jnp.zeros(acc_ref.shape, F32)

    @pl.when(j < npg)
    def _():
        x = indexer(jnp.concatenate([r[...] for r in kidx_refs], axis=1))
        sc_ref[j] = x
        mm_ref[0] = jnp.minimum(mm_ref[0], fold_lanes(x, jnp.minimum))
        mm_ref[1] = jnp.maximum(mm_ref[1], fold_lanes(x, jnp.maximum))

    @pl.when(j == npg - 1)
    def _():
        kt_new = jnp.transpose(padded(sm_ref[...]))[SM_KI:SM_KI + IDX_DIM]
        x = indexer(kt_new)
        qrow = lax.broadcasted_iota(I32, (rows, ps), 0)
        kcol = lax.broadcasted_iota(I32, (rows, ps), 1)
        adm = (kcol <= qrow) & (kcol < t_real)
        sc_ref[npg] = jnp.full(sc_ref.shape[1:], -jnp.inf, F32)
        sc_ref[npg, :, 0:ps] = jnp.where(adm, x, -jnp.inf)
        mn = jnp.min(jnp.minimum(mm_ref[0], jnp.where(adm, x, jnp.inf)), axis=1, keepdims=True)
        mx = jnp.max(jnp.maximum(mm_ref[1], jnp.where(adm, x, -jnp.inf)), axis=1, keepdims=True)

        def count_ge(t):
            def body(b, cnt):
                return cnt + fold_lanes(jnp.where(sc_ref[b] >= t, 1.0, 0.0), jnp.add)
            cnt = lax.fori_loop(0, npg + 1, body, jnp.zeros((rows, LANES), F32))
            return jnp.sum(cnt, axis=1, keepdims=True)

        q1 = lax.broadcasted_iota(I32, (rows, 1), 0)
        n_adm = (past_len + jnp.minimum(q1, t_real - 1) + 1).astype(F32)
        thr = _kth_threshold(count_ge, mn, mx, n_adm, topk)
        thr_ref[...] = lane_form(jnp.broadcast_to(thr, (rows, LANES)))[0:rows]

    @pl.when(j >= npg)
    def _():
        bias = jnp.where(lane_form(sc_ref[j - npg]) >= thr_ref[0:1, :], 0.0, MASK_BIAS)
        s = bias
        for g in range(A_KV_HEADS):
            kg = jnp.concatenate([r[pl.ds(g, ps, stride=A_KV_HEADS), :] for r in k_refs], axis=0)
            s = s + jnp.dot(kg.astype(BF16), qt_ref[g], preferred_element_type=F32)
        softmax_step(s, lambda g: jnp.concatenate([r[pl.ds(g, ps, stride=A_KV_HEADS), :] for r in v_refs], axis=0))

    @pl.when(j == 2 * npg - 1)
    def _():
        s = jnp.where(lane_form(sc_ref[npg, :, 0:ps]) >= thr_ref[0:1, :], 0.0, MASK_BIAS)
        for g in range(A_KV_HEADS):
            kg = padded(kn_ref[:, g * HEAD_DIM:(g + 1) * HEAD_DIM])
            s = s + jnp.dot(kg.astype(BF16), qt_ref[g], preferred_element_type=F32)
        softmax_step(s, lambda g: padded(vn_ref[:, g * HEAD_DIM:(g + 1) * HEAD_DIM]))
        for g in range(A_KV_HEADS):
            out = jnp.transpose(acc_ref[g] / l_ref[0:1, :])
            for h in range(g * hpg, (g + 1) * hpg):
                o_ref[:, h * HEAD_DIM:(h + 1) * HEAD_DIM] = out[h * rows:(h + 1) * rows, :].astype(BF16)


def _dsa_sample(packed3, cache_k, cache_v, cache_kidx, page_table, t_real, heads):
    nb, rows, _ = packed3.shape
    assert rows == SUBLANES and IDX_HEADS * rows == LANES
    ps = cache_k.shape[2]
    assert ps == LANES and cache_k.shape[0] == 1
    n_pages = page_table.shape[1]
    pg = min(SAMPLE_PAGES_PER_STEP, n_pages)
    npg = n_pages // pg
    assert n_pages % pg == 0
    kvw = A_KV_HEADS * HEAD_DIM
    aw = heads * HEAD_DIM
    iw = IDX_HEADS * IDX_DIM
    past_len = n_pages * ps
    topk = min(TOPK_MAX, (past_len + t_real) // 4)
    pt = page_table.reshape(-1).astype(I32)

    def page1(b, j, pt, i):
        return pt[b * n_pages + jnp.minimum(j, npg - 1) * pg + i]

    def page2(b, j, pt, i):
        return pt[b * n_pages + jnp.maximum(j - npg, 0) * pg + i]

    kidx_t = jnp.swapaxes(cache_kidx, 2, 3)

    def kidx_spec(i):
        return pl.BlockSpec((None, None, IDX_DIM, ps), lambda b, j, pt: (0, page1(b, j, pt, i), 0, 0))

    ck = cache_k.reshape(cache_k.shape[1], ps * A_KV_HEADS, HEAD_DIM)
    cv = cache_v.reshape(cache_v.shape[1], ps * A_KV_HEADS, HEAD_DIM)

    def kv_spec(i):
        return pl.BlockSpec((None, ps * A_KV_HEADS, HEAD_DIM),
                            lambda b, j, pt: (page2(b, j, pt, i), 0, 0))

    in_specs = [pl.BlockSpec((None, rows, aw), lambda b, j, pt: (b, 0, QA_OFF // aw)),
                pl.BlockSpec((None, rows, iw), lambda b, j, pt: (b, 0, QI_OFF // iw)),
                pl.BlockSpec((None, rows, LANES), lambda b, j, pt: (b, 0, SM_OFF // LANES)),
                pl.BlockSpec((None, rows, kvw), lambda b, j, pt: (b, 0, KA_OFF // kvw)),
                pl.BlockSpec((None, rows, kvw), lambda b, j, pt: (b, 0, VA_OFF // kvw))]
    in_specs += [kidx_spec(i) for i in range(pg)]
    in_specs += [kv_spec(i) for i in range(pg)]
    in_specs += [kv_spec(i) for i in range(pg)]
    return pl.pallas_call(
        functools.partial(_dsa_sample_kernel, pg=pg, npg=npg, past_len=past_len, t_real=t_real,
                          topk=topk, heads=heads),
        out_shape=jax.ShapeDtypeStruct((nb, rows, aw), BF16),
        grid_spec=pltpu.PrefetchScalarGridSpec(
            num_scalar_prefetch=1,
            grid=(nb, 2 * npg),
            in_specs=in_specs,
            out_specs=pl.BlockSpec((None, rows, aw), lambda b, j, pt: (b, 0, 0)),
            scratch_shapes=[pltpu.VMEM((IDX_HEADS * rows, IDX_DIM), BF16),
                            pltpu.VMEM((IDX_HEADS * rows, LANES), F32),
                            pltpu.VMEM((A_KV_HEADS, HEAD_DIM, LANES), BF16),
                            pltpu.VMEM((npg + 1, rows, pg * ps), F32),
                            pltpu.VMEM((2, rows, LANES), F32),
                            pltpu.VMEM((rows, LANES), F32),
                            pltpu.VMEM((LANES, LANES), F32),
                            pltpu.VMEM((rows, LANES), F32),
                            pltpu.VMEM((rows, LANES), F32),
                            pltpu.VMEM((A_KV_HEADS, HEAD_DIM, LANES), F32)]),
        compiler_params=_cparams("arbitrary", "arbitrary"),
    )(pt, packed3, packed3, packed3, packed3, packed3,
      *([kidx_t] * pg), *([ck] * pg), *([cv] * pg))


def _split3(x):
    x1 = x.astype(BF16)
    r = x - x1.astype(F32)
    x2 = r.astype(BF16)
    return x1, x2, (r - x2.astype(F32)).astype(BF16)


def _exact_right(x, m01):
    d = functools.partial(jnp.dot, preferred_element_type=F32)
    x1, x2, x3 = _split3(x)
    return (d(x1, m01) + d(x2, m01)) + d(x3, m01)


def _exact_left(m01, x):
    d = functools.partial(jnp.dot, preferred_element_type=F32)
    x1, x2, x3 = _split3(x)
    return (d(m01, x1) + d(m01, x2)) + d(m01, x3)


def _dot3s(ah, al, bh, bl):
    d = functools.partial(jnp.dot, preferred_element_type=F32)
    return d(ah, bh) + (d(ah, bl) + d(al, bh))


def _blockdiag_mask(nb, rb, cb):
    shape = (nb * rb, nb * cb)
    return lax.broadcasted_iota(I32, shape, 0) // rb == lax.broadcasted_iota(I32, shape, 1) // cb


def _blockdiag(x, mask):
    t = jnp.tile(x, (mask.shape[0] // x.shape[0], 1))
    return jnp.where(mask, t, jnp.zeros_like(t))


def _unit_lower_inverse_cat(a_cats, c, mask):
    shape = a_cats[0].shape
    eye = (lax.broadcasted_iota(I32, shape, 0) == lax.broadcasted_iota(I32, shape, 1) % c).astype(F32)

    def bd_halves(p):
        ph, pl_ = _split(p)
        return ph, pl_, _blockdiag(ph, mask), _blockdiag(pl_, mask)

    xs = [eye - a for a in a_cats]
    ps = []
    for a in a_cats:
        ph, pl_, bh, bl = bd_halves(a)
        ps.append(_dot3s(ph, pl_, bh, bl))
    n = 2
    while n < c:
        nxt_x, nxt_p = [], []
        for x, p in zip(xs, ps):
            _, _, bh, bl = bd_halves(p)
            if 2 * n < c:
                lh, ll = _split(jnp.concatenate([x, p], axis=0))
                r = _dot3s(lh, ll, bh, bl)
                nxt_x.append(x + r[0:c])
                nxt_p.append(r[c:2 * c])
            else:
                xh, xl = _split(x)
                nxt_x.append(x + _dot3s(xh, xl, bh, bl))
                nxt_p.append(p)
        xs, ps = nxt_x, nxt_p
        n *= 2
    return xs


def _delta_kernel(qb_ref, kb_ref, vb_ref, z_ref, sm_ref, conv0_ref, wconv_ref, alog_ref, dtb_ref,
                  wnorm_ref, ssm0_ref, eb_ref, eg_ref, egc_ref, o_ref, convo_ref, ssmo_ref, xbuf_ref, s_ref,
                  *, tci, tcc, t_real, heads):
    j = pl.program_id(1)
    c = DELTA_CHUNK
    hw = heads * B_KEY_DIM
    halo = SUBLANES
    hg = DELTA_GROUP
    gw, cw, pw = hg * B_KEY_DIM, hg * c, 2 * B_KEY_DIM

    @pl.when(j == 0)
    def _():
        xbuf_ref[0:halo, :] = conv0_ref[...]
        s_ref[...] = jnp.zeros(s_ref.shape, F32)
        for h in range(heads):
            o = (h % 2) * B_KEY_DIM
            s_ref[h // 2, o:o + B_KEY_DIM, o:o + B_VAL_DIM] = ssm0_ref[h]

    @pl.when(j > 0)
    def _():
        xbuf_ref[0:halo, :] = xbuf_ref[tcc:tcc + halo, :]

    xbuf_ref[halo:halo + tci, 0:hw] = qb_ref[...]
    xbuf_ref[halo:halo + tci, hw:2 * hw] = kb_ref[...]
    xbuf_ref[halo:halo + tci, 2 * hw:3 * hw] = vb_ref[...]
    if tci < tcc:
        xbuf_ref[halo + tci:halo + tcc, :] = jnp.zeros((tcc - tci, 3 * hw), F32)

    jr, r1 = (t_real - 1) // tcc, t_real - ((t_real - 1) // tcc) * tcc

    @pl.when(j == jr)
    def _():
        convo_ref[...] = jnp.zeros(convo_ref.shape, F32)
        convo_ref[halo - (CONV_WIDTH - 1):halo, :] = xbuf_ref[halo + r1 - (CONV_WIDTH - 1):halo + r1, :]

    conv = xbuf_ref[halo - 3:halo - 3 + tcc, :] * wconv_ref[0:1, :]
    for t in range(1, CONV_WIDTH):
        conv = conv + xbuf_ref[halo - 3 + t:halo - 3 + t + tcc, :] * wconv_ref[t:t + 1, :]
    act = _silu(conv)

    sm = sm_ref[...]
    if tci < tcc:
        sm = jnp.concatenate([sm, jnp.zeros((tcc - tci, LANES), F32)], axis=0)
    lane = lax.broadcasted_iota(I32, (tcc, LANES), 1)
    trow = j * tcc + lax.broadcasted_iota(I32, (tcc, LANES), 0)
    valid = trow < t_real
    beta_t = jnp.where(valid & (lane >= SM_BETA) & (lane < SM_BETA + heads), _sigmoid(sm), 0.0)
    xg = sm + dtb_ref[...]
    softplus = jnp.maximum(xg, 0.0) + jnp.log1p(jnp.exp(-jnp.abs(xg)))
    lg_t = jnp.where(valid & (lane >= SM_DECAY) & (lane < SM_DECAY + heads),
                     -jnp.exp(alog_ref[...]) * softplus, 0.0)
    ri = lax.broadcasted_iota(I32, (tcc, tcc), 0)
    ci = lax.broadcasted_iota(I32, (tcc, tcc), 1)
    tri = ((ci <= ri) & (ci // c == ri // c)).astype(BF16)
    g_t = _exact_left(tri, lg_t)
    rmask = valid[:, 0:1]

    def l2norm_heads(x, scale):
        parts = []
        for h in range(heads):
            xh = x[:, h * B_KEY_DIM:(h + 1) * B_KEY_DIM]
            parts.append(xh * (lax.rsqrt(jnp.sum(xh * xh, axis=-1, keepdims=True) + L2_EPS) * scale))
        return jnp.concatenate(parts, axis=1)

    qn_all = l2norm_heads(act[:, 0:hw], float(B_KEY_DIM ** -0.5))
    kn_all = l2norm_heads(act[:, hw:2 * hw], 1.0)
    v_all = act[:, 2 * hw:3 * hw]
    if t_real % tcc != 0 or tci < tcc:
        kn_all = jnp.where(rmask, kn_all, 0.0)
        v_all = jnp.where(rmask, v_all, 0.0)
    beta_e = _exact_right(beta_t, eb_ref[...])
    g_e = _exact_right(g_t, eg_ref[...])
    gc_all = _exact_right(g_t, egc_ref[...])
    eg_e = jnp.exp(g_e)
    kbeta_all = kn_all * beta_e
    vbeta_all = v_all * beta_e
    kbe_all = kbeta_all * eg_e
    qe_all = qn_all * eg_e

    crow = lax.broadcasted_iota(I32, (c, heads * c), 0)
    ccol = lax.broadcasted_iota(I32, (c, heads * c), 1) % c
    strict = (lax.broadcasted_iota(I32, (c, cw), 0) > lax.broadcasted_iota(I32, (c, cw), 1) % c)
    ones_cc = jnp.ones((c, c), BF16)
    assert B_KEY_DIM == B_VAL_DIM
    mask_c = _blockdiag_mask(hg, c, c)
    mask_d = _blockdiag_mask(hg, c, B_KEY_DIM)
    prow = lax.broadcasted_iota(I32, (pw, pw), 0) // B_KEY_DIM
    pcol = lax.broadcasted_iota(I32, (pw, pw), 1) // B_VAL_DIM
    units = [(ch, grp) for ch in range(tcc // c) for grp in range(heads // hg)]
    a_cats, qk_cats = [], []
    for ch in range(tcc // c):
        r = slice(ch * c, (ch + 1) * c)
        gc = gc_all[r]
        grow = _exact_left(ones_cc, jnp.where(crow == ccol, gc, 0.0))
        decay = jnp.where(crow >= ccol, jnp.exp(jnp.minimum(gc - grow, 0.0)), 0.0)
        for grp in range(heads // hg):
            ls = slice(grp * gw, (grp + 1) * gw)
            dec = decay[:, grp * cw:(grp + 1) * cw]
            kn_bd = _blockdiag(kn_all[r, ls].astype(BF16), mask_d)
            lhs = jnp.concatenate([kbeta_all[r, ls], qn_all[r, ls]], axis=0).astype(BF16)
            aq = lax.dot_general(lhs, kn_bd, NT_DIMS, preferred_element_type=F32)
            a_cats.append(jnp.where(strict, aq[0:c] * dec, 0.0))
            qk_cats.append(aq[c:2 * c] * dec)
    tinvs = _unit_lower_inverse_cat(a_cats, c, mask_c)
    us, ws = [], []
    for (ch, grp), tinv in zip(units, tinvs):
        r = slice(ch * c, (ch + 1) * c)
        ls = slice(grp * gw, (grp + 1) * gw)
        th, tl = _split(tinv)
        vh_, vl_ = _split(vbeta_all[r, ls])
        kh_, kl_ = _split(kbe_all[r, ls])
        us.append(_dot3s(th, tl, _blockdiag(vh_, mask_d), _blockdiag(vl_, mask_d)))
        ws.append(_dot3s(th, tl, _blockdiag(kh_, mask_d), _blockdiag(kl_, mask_d)))

    for ch in range(tcc // c):
        r = slice(ch * c, (ch + 1) * c)
        glast = g_e[(ch + 1) * c - 1:(ch + 1) * c, :]
        kdec_all = kn_all[r] * jnp.exp(glast - g_e[r])
        eglast = jnp.exp(glast)
        for grp in range(heads // hg):
            un = units.index((ch, grp))
            u, w, qk_cat = us[un], ws[un], qk_cats[un]
            vnew, qs = [], []
            for pr in range(hg // 2):
                idx = grp * (hg // 2) + pr
                la = slice(grp * gw + pr * pw, grp * gw + (pr + 1) * pw)
                lg_ = slice(pr * pw, (pr + 1) * pw)
                s_old = s_ref[idx]
                lhs2 = jnp.concatenate([w[:, lg_], qe_all[r, la]], axis=0).astype(BF16)
                rs = jnp.dot(lhs2, s_old.astype(BF16), preferred_element_type=F32)
                vn = u[:, lg_] - rs[0:c]
                upd = lax.dot_general(kdec_all[:, la].astype(BF16), vn.astype(BF16), TN_DIMS,
                                      preferred_element_type=F32)
                s_ref[idx] = s_old * eglast[:, la] + jnp.where(prow == pcol, upd, 0.0)
                vnew.append(vn)
                qs.append(rs[c:2 * c])
            vn_bd = _blockdiag(jnp.concatenate(vnew, axis=1).astype(BF16), mask_d)
            o_g = jnp.concatenate(qs, axis=1) + jnp.dot(qk_cat.astype(BF16), vn_bd, preferred_element_type=F32)
            if ch * c < tci:
                ro = slice(ch * c, min((ch + 1) * c, tci))
                nr = ro.stop - ro.start
                for a in range(hg):
                    h = grp * hg + a
                    oh = o_g[0:nr, a * B_VAL_DIM:(a + 1) * B_VAL_DIM]
                    zz = z_ref[ro, h * B_VAL_DIM:(h + 1) * B_VAL_DIM]
                    on = oh * lax.rsqrt(jnp.mean(oh * oh, axis=-1, keepdims=True) + RMS_EPS)
                    o_ref[ro, h * B_VAL_DIM:(h + 1) * B_VAL_DIM] = (on * wnorm_ref[...] * _silu(zz)).astype(BF16)

    @pl.when(j == pl.num_programs(1) - 1)
    def _():
        for h in range(heads):
            o = (h % 2) * B_KEY_DIM
            ssmo_ref[h] = s_ref[h // 2, o:o + B_KEY_DIM, o:o + B_VAL_DIM]


def _delta_net(packed, conv0, ssm0, w_conv, a_log, dt_bias, w_onorm, batch, t_pad, t_real, tci, tcc, heads):
    hw = heads * B_KEY_DIM
    nt = t_pad // tci
    assert t_pad % tci == 0 and tcc % DELTA_CHUNK == 0 and tci % SUBLANES == 0
    assert (nt == 1 and tci <= tcc) or tci == tcc
    assert heads % DELTA_GROUP == 0 and DELTA_GROUP % 2 == 0
    wconv_p = jnp.pad(w_conv, ((0, SUBLANES - CONV_WIDTH), (0, 0)))
    alog_row = jnp.zeros((1, LANES), F32).at[0, SM_DECAY:SM_DECAY + heads].set(a_log)
    dtb_row = jnp.zeros((1, LANES), F32).at[0, SM_DECAY:SM_DECAY + heads].set(dt_bias)
    def spread(src, width):
        m = np.zeros((LANES, heads * width), np.float32)
        for h in range(heads):
            m[src + h, h * width:(h + 1) * width] = 1.0
        return jnp.asarray(m, BF16)
    e_beta, e_g, e_gc = spread(SM_BETA, B_KEY_DIM), spread(SM_DECAY, B_KEY_DIM), spread(SM_DECAY, DELTA_CHUNK)
    full2 = lambda a: pl.BlockSpec(a.shape, lambda b, j: (0, 0))
    row = lambda b, j: (b * nt + j)
    return pl.pallas_call(
        functools.partial(_delta_kernel, tci=tci, tcc=tcc, t_real=t_real, heads=heads),
        out_shape=(jax.ShapeDtypeStruct((batch * t_pad, hw), BF16),
                   jax.ShapeDtypeStruct((batch, SUBLANES, 3 * hw), F32),
                   jax.ShapeDtypeStruct((batch, heads, B_KEY_DIM, B_VAL_DIM), F32)),
        grid=(batch, nt),
        in_specs=[pl.BlockSpec((tci, hw), lambda b, j: (row(b, j), QB_OFF // hw)),
                  pl.BlockSpec((tci, hw), lambda b, j: (row(b, j), KB_OFF // hw)),
                  pl.BlockSpec((tci, hw), lambda b, j: (row(b, j), VB_OFF // hw)),
                  pl.BlockSpec((tci, hw), lambda b, j: (row(b, j), Z_OFF // hw)),
                  pl.BlockSpec((tci, LANES), lambda b, j: (row(b, j), SM_OFF // LANES)),
                  pl.BlockSpec((None, SUBLANES, 3 * hw), lambda b, j: (b, 0, 0)),
                  pl.BlockSpec((SUBLANES, 3 * hw), lambda b, j: (0, 0)),
                  pl.BlockSpec((1, LANES), lambda b, j: (0, 0)),
                  pl.BlockSpec((1, LANES), lambda b, j: (0, 0)),
                  pl.BlockSpec((1, B_VAL_DIM), lambda b, j: (0, 0)),
                  pl.BlockSpec((None, heads, B_KEY_DIM, B_VAL_DIM), lambda b, j: (b, 0, 0, 0)),
                  full2(e_beta), full2(e_g), full2(e_gc)],
        out_specs=(pl.BlockSpec((tci, hw), lambda b, j: (row(b, j), 0)),
                   pl.BlockSpec((None, SUBLANES, 3 * hw), lambda b, j: (b, 0, 0)),
                   pl.BlockSpec((None, heads, B_KEY_DIM, B_VAL_DIM), lambda b, j: (b, 0, 0, 0))),
        scratch_shapes=[pltpu.VMEM((tcc + SUBLANES, 3 * hw), F32),
                        pltpu.VMEM((heads // 2, 2 * B_KEY_DIM, 2 * B_VAL_DIM), F32)],
        compiler_params=_cparams("arbitrary", "arbitrary"),
    )(packed, packed, packed, packed, packed, conv0, wconv_p, alog_row, dtb_row,
      w_onorm.reshape(1, B_VAL_DIM), ssm0, e_beta, e_g, e_gc)


def _outproj_kernel(oa_ref, ob_ref, x_ref, wa_ref, wb_ref, gt_ref, sh_ref, sc_ref, g_ref, b_ref,
                    wr_ref, br_ref, x1_ref, u2_ref, lg_ref, *, alpha):
    mix = (jnp.dot(oa_ref[...], wa_ref[...], preferred_element_type=F32)
           + jnp.dot(ob_ref[...], wb_ref[...], preferred_element_type=F32))
    x1 = _ln(alpha * x_ref[...] + gt_ref[...] * mix) * g_ref[...] + b_ref[...]
    x1_ref[...] = x1
    u2 = _ln(x1) * (1.0 + sc_ref[...]) + sh_ref[...]
    u2_ref[...] = u2.astype(BF16)
    lg_ref[...] = _dot3(u2, wr_ref[...]) + br_ref[...]


def _out_projection(oa, ob, x, wa, wb, gt, sh, sc, ln_g, ln_b, w_r, b_r, rows_per_group, tm, alpha):
    m, d = x.shape
    hwid = oa.shape[1]
    if gt.ndim == 3:
        tpg = rows_per_group // tm
        mod_spec = pl.BlockSpec((None, 1, d), lambda i: (i // tpg, 0, 0))
    else:
        mod_spec = pl.BlockSpec((tm, d), lambda i: (i, 0))
    full = lambda r, c: pl.BlockSpec((r, c), lambda i: (0, 0))
    return pl.pallas_call(
        functools.partial(_outproj_kernel, alpha=alpha),
        out_shape=(jax.ShapeDtypeStruct((m, d), F32), jax.ShapeDtypeStruct((m, d), BF16),
                   jax.ShapeDtypeStruct((m, LANES), F32)),
        grid=(m // tm,),
        in_specs=[pl.BlockSpec((tm, hwid), lambda i: (i, 0)), pl.BlockSpec((tm, hwid), lambda i: (i, 0)),
                  pl.BlockSpec((tm, d), lambda i: (i, 0)), full(hwid, d), full(hwid, d),
                  mod_spec, mod_spec, mod_spec, full(1, d), full(1, d), full(d, LANES), full(1, LANES)],
        out_specs=(pl.BlockSpec((tm, d), lambda i: (i, 0)), pl.BlockSpec((tm, d), lambda i: (i, 0)),
                   pl.BlockSpec((tm, LANES), lambda i: (i, 0))),
        compiler_params=_cparams("arbitrary"),
    )(oa, ob, x, wa, wb, gt, sh, sc, ln_g.reshape(1, d), ln_b.reshape(1, d), w_r, b_r)


def _combine_weights(lg):
    lane = lax.broadcasted_iota(I32, lg.shape, 1)
    big = jnp.int32(LANES)
    gmask = lane < N_GROUPS
    gl = jnp.where(gmask, lg, -jnp.inf)
    gmax = jnp.max(gl, axis=-1, keepdims=True)
    g_idx = jnp.min(jnp.where(gl == gmax, lane, big), axis=-1, keepdims=True)
    p_top = 1.0 / jnp.sum(jnp.where(gmask, jnp.exp(gl - gmax), 0.0), axis=-1, keepdims=True)
    e_lane = lane - RT_EXP
    in_grp = (e_lane >= 0) & (e_lane < N_EXPERTS) & ((e_lane // EXPERTS_PER_GROUP) == g_idx)
    v = jnp.where(in_grp, lg, -jnp.inf)
    v1 = jnp.max(v, axis=-1, keepdims=True)
    i1 = jnp.min(jnp.where(v == v1, lane, big), axis=-1, keepdims=True)
    vr = jnp.where(lane == i1, -jnp.inf, v)
    v2 = jnp.max(vr, axis=-1, keepdims=True)
    i2 = jnp.min(jnp.where(vr == v2, lane, big), axis=-1, keepdims=True)
    e2 = jnp.exp(v2 - v1)
    den = 1.0 / (1.0 + e2)
    return jnp.where(lane == i1, den * p_top, jnp.where(lane == i2, e2 * den * p_top, 0.0))


def _moe_kernel(u_ref, lg_ref, x1_ref, wg_ref, wu_ref, wd_ref, gt_ref, g_ref, b_ref, o_ref,
                acc_ref, cmb_ref, *, alpha):
    e = pl.program_id(1)

    @pl.when(e == 0)
    def _():
        cmb_ref[...] = _combine_weights(lg_ref[...])
        acc_ref[...] = jnp.zeros(acc_ref.shape, F32)

    u = u_ref[...]
    h = jnp.dot(u, wg_ref[...], preferred_element_type=F32)
    up = jnp.dot(u, wu_ref[...], preferred_element_type=F32)
    lane = lax.broadcasted_iota(I32, cmb_ref.shape, 1)
    c_e = jnp.sum(jnp.where(lane == e + RT_EXP, cmb_ref[...], 0.0), axis=-1, keepdims=True)
    act = _silu(h) * up * c_e
    acc_ref[...] += jnp.dot(act.astype(BF16), wd_ref[...], preferred_element_type=F32)

    @pl.when(e == pl.num_programs(1) - 1)
    def _():
        o_ref[...] = _ln(alpha * x1_ref[...] + gt_ref[...] * acc_ref[...]) * g_ref[...] + b_ref[...]


def _moe(u2, logits, x1, wg, wu, wd, gt, ln_g, ln_b, rows_per_group, tm, alpha):
    m, d = x1.shape
    ne, _, de = wg.shape
    if gt.ndim == 3:
        tpg = rows_per_group // tm
        mod_spec = pl.BlockSpec((None, 1, d), lambda i, e: (i // tpg, 0, 0))
    else:
        mod_spec = pl.BlockSpec((tm, d), lambda i, e: (i, 0))
    return pl.pallas_call(
        functools.partial(_moe_kernel, alpha=alpha),
        out_shape=jax.ShapeDtypeStruct((m, d), F32),
        grid=(m // tm, ne),
        in_specs=[pl.BlockSpec((tm, d), lambda i, e: (i, 0)),
                  pl.BlockSpec((tm, LANES), lambda i, e: (i, 0)),
                  pl.BlockSpec((tm, d), lambda i, e: (i, 0)),
                  pl.BlockSpec((None, d, de), lambda i, e: (e, 0, 0)),
                  pl.BlockSpec((None, d, de), lambda i, e: (e, 0, 0)),
                  pl.BlockSpec((None, de, d), lambda i, e: (e, 0, 0)),
                  mod_spec,
                  pl.BlockSpec((1, d), lambda i, e: (0, 0)),
                  pl.BlockSpec((1, d), lambda i, e: (0, 0))],
        out_specs=pl.BlockSpec((tm, d), lambda i, e: (i, 0)),
        scratch_shapes=[pltpu.VMEM((tm, d), F32), pltpu.VMEM((tm, LANES), F32)],
        compiler_params=_cparams("arbitrary", "arbitrary"),
    )(u2, logits, x1, wg, wu, wd, gt, ln_g.reshape(1, d), ln_b.reshape(1, d))


def _layer(x2d, mod, groups, rows_per_group, t_real, pos, w, attend, conv0, ssm0, tm, delta_tiles):
    m, d = x2d.shape
    heads = d // (2 * HEAD_DIM)
    alpha = float((2 * 1) ** 0.25)
    sh1, sc1, gt1, sh2, sc2, gt2 = jnp.split(mod, 6, axis=-1)
    if rows_per_group % tm == 0:
        expand = lambda a: a[:, None, :]
    else:
        expand = lambda a: jnp.repeat(a, rows_per_group, axis=0)
    sh1, sc1, gt1, sh2, sc2, gt2 = [expand(a) for a in (sh1, sc1, gt1, sh2, sc2, gt2)]
    c128, s128 = _rope_tables(pos, HEAD_DIM // 2)
    c64, s64 = _rope_tables(pos, IDX_DIM // 2)
    tabs = (c128, s128, c64, s64)
    if rows_per_group % tm != 0:
        tabs = tuple(jnp.tile(t, (tm // rows_per_group, 1)) for t in tabs)
    packed = _in_projection(x2d, sh1, sc1, w["w_in"], tabs, rows_per_group, tm)
    o_a = attend(packed)
    tci, tcc = delta_tiles
    o_b, conv_new, ssm_new = _delta_net(packed, conv0, ssm0, w["w_conv"], w["a_log"], w["dt_bias"],
                                        w["w_onorm"], groups, rows_per_group, t_real, tci, tcc, heads)
    x1, u2, logits = _out_projection(o_a, o_b, x2d, w["w_out_a"], w["w_out_b"], gt1, sh2, sc2,
                                     w["ln1_g"], w["ln1_b"], w["w_router"], w["b_router"],
                                     rows_per_group, min(tm, OUTPROJ_TM), alpha)
    x2 = _moe(u2, logits, x1, w["w_gate"], w["w_up"], w["w_down"], gt2, w["ln2_g"], w["ln2_b"],
              rows_per_group, min(tm, MOE_TM), alpha)
    return x2, packed, conv_new, ssm_new


def kernel(x_prompt, x_sample, cache_k, cache_v, cache_kidx, state_conv, state_ssm, page_table,
           c_prompt, c_sample, w_mod, b_mod, w_in, w_conv, a_log, dt_bias, w_onorm, w_out,
           ln1_g, ln1_b, w_grp, b_grp, w_erouter, b_erouter, w_gate, w_up, w_down, ln2_g, ln2_b):
    assert w_mod.shape[0] == 1, "single layer"
    batch, seq, d = x_prompt.shape
    nb, dec_seq, _ = x_sample.shape
    heads = d // (2 * HEAD_DIM)
    hw = heads * B_KEY_DIM
    past_len = page_table.shape[1] * cache_k.shape[2]
    kvw = A_KV_HEADS * HEAD_DIM

    n_rt = N_GROUPS + N_EXPERTS
    w_router = jnp.pad(jnp.concatenate([w_grp[0], w_erouter[0]], axis=1), ((0, 0), (0, LANES - n_rt)))
    b_router = jnp.pad(jnp.concatenate([b_grp[0], b_erouter[0]]), (0, LANES - n_rt)).reshape(1, LANES)
    w = dict(w_in=_pack_w_in(w_in[0], d), w_conv=w_conv[0], a_log=a_log[0], dt_bias=dt_bias[0],
             w_onorm=w_onorm[0], w_out_a=w_out[0, :hw].astype(BF16), w_out_b=w_out[0, hw:].astype(BF16),
             ln1_g=ln1_g[0], ln1_b=ln1_b[0], w_router=w_router, b_router=b_router,
             w_gate=w_gate[0].astype(BF16), w_up=w_up[0].astype(BF16), w_down=w_down[0].astype(BF16),
             ln2_g=ln2_g[0], ln2_b=ln2_b[0])

    n_c = batch + nb
    c_all = jnp.pad(jnp.concatenate([c_prompt, c_sample], axis=0), ((0, (-n_c) % SUBLANES), (0, 0)))
    mod = _modulation(c_all, w_mod[0], b_mod[0])

    def attend_prompt(packed):
        k_bf = packed[:, KA_OFF:KA_OFF + kvw].astype(BF16)
        v_bf = packed[:, VA_OFF:VA_OFF + kvw].astype(BF16)
        ki_bf = packed[:, SM_OFF + SM_KI:SM_OFF + SM_KI + IDX_DIM].astype(BF16)
        return _dsa_prompt(packed, k_bf, v_bf, ki_bf, batch, seq, heads)

    tm_p = min(INPROJ_TM, seq)
    conv0_p = jnp.zeros((batch, SUBLANES, 3 * hw), F32)
    ssm0_p = jnp.zeros((batch, heads, B_KEY_DIM, B_VAL_DIM), F32)
    tc_p = min(DELTA_TM, seq)
    y_p, packed_p, conv_p, ssm_p = _layer(
        x_prompt.reshape(batch * seq, d), mod[:batch], batch, seq, seq, jnp.arange(seq, dtype=I32), w,
        attend_prompt, conv0_p, ssm0_p, tm_p, (tc_p, tc_p))

    rows = SUBLANES
    assert CONV_WIDTH - 1 <= dec_seq <= rows
    x_s = jnp.pad(x_sample, ((0, 0), (0, rows - dec_seq), (0, 0))).reshape(nb * rows, d)
    pos_s = past_len + jnp.arange(rows, dtype=I32)

    def attend_sample(packed):
        o = _dsa_sample(packed.reshape(nb, rows, PACK_W), cache_k, cache_v, cache_kidx,
                        page_table, dec_seq, heads)
        return o.reshape(nb * rows, heads * HEAD_DIM)

    conv0_s = jnp.pad(state_conv[0], ((0, 0), (rows - (CONV_WIDTH - 1), 0), (0, 0)))
    tm_s = min(256, nb * rows)
    y_s, packed_s, conv_s, ssm_s = _layer(
        x_s, mod[batch:batch + nb], nb, rows, dec_seq, pos_s, w, attend_sample, conv0_s, state_ssm[0],
        tm_s, (rows, DELTA_CHUNK))

    def states(packed, groups, t_pad, t):
        p = packed.reshape(groups, t_pad, PACK_W)[:, :t]
        k = p[..., KA_OFF:KA_OFF + kvw].reshape(1, groups, t, A_KV_HEADS, HEAD_DIM)
        v = p[..., VA_OFF:VA_OFF + kvw].reshape(1, groups, t, A_KV_HEADS, HEAD_DIM)
        ki = p[..., SM_OFF + SM_KI:SM_OFF + SM_KI + IDX_DIM][None]
        return k, v, ki

    k_p, v_p, ki_p = states(packed_p, batch, seq, seq)
    k_s, v_s, ki_s = states(packed_s, nb, rows, dec_seq)
    conv_p = conv_p[None, :, rows - (CONV_WIDTH - 1):]
    conv_s = conv_s[None, :, rows - (CONV_WIDTH - 1):]
    y_s = y_s.reshape(nb, rows, d)[:, :dec_seq]
    return (y_p.reshape(batch, seq, d), y_s, k_p, v_p, ki_p, conv_p, ssm_p[None],
            k_s, v_s, ki_s, conv_s, ssm_s[None])
```

```python
import functools

import numpy as np
import jax
import jax.numpy as jnp
from jax import lax
from jax.experimental import pallas as pl
from jax.experimental.pallas import tpu as pltpu

F32 = jnp.float32
BF16 = jnp.bfloat16
I32 = jnp.int32

HEAD_DIM = 128
A_KV_HEADS = 2
IDX_HEADS = 16
IDX_DIM = 64
TOPK_MAX = 256
ROPE_THETA = 10000.0
B_KEY_DIM = 128
B_VAL_DIM = 128
CONV_WIDTH = 4
DELTA_CHUNK = 64
DELTA_GROUP = 4
N_GROUPS = 4
EXPERTS_PER_GROUP = 4
N_EXPERTS = N_GROUPS * EXPERTS_PER_GROUP
LN_EPS = 1e-5
RMS_EPS = 1e-6
L2_EPS = 1e-6

LANES = 128
SUBLANES = 8
VMEM_LIMIT = 56 * 1024 * 1024
INPROJ_TM = 1024
OUTPROJ_TM = 256
MOE_TM = 512
DELTA_TM = 256
DSA_TQ = 256
SAMPLE_PAGES_PER_STEP = 32

PACK_TN = 512
QA_OFF, QI_OFF, QB_OFF, KB_OFF, VB_OFF, Z_OFF, KA_OFF, VA_OFF, SM_OFF = (
    0, 1024, 2048, 3072, 4096, 5120, 6144, 6400, 6656)
PACK_W = 7168
SM_KI, SM_WI, SM_BETA, SM_DECAY = 0, 64, 80, 88
RT_GRP, RT_EXP = 0, N_GROUPS

NEG_BIG = -1e30
MASK_BIAS = -2e30
MAX_SEARCH_ITERS = 64
LOG2E = 1.4426950408889634
EXP2_SAFE_BOUND = 60.0
BOUND_SLACK = 1.02

NT_DIMS = (((1,), (1,)), ((), ()))
TN_DIMS = (((0,), (0,)), ((), ()))


def _cparams(*sem):
    return pltpu.CompilerParams(dimension_semantics=sem, vmem_limit_bytes=VMEM_LIMIT)


def _ln(x):
    mu = jnp.mean(x, axis=-1, keepdims=True)
    xc = x - mu
    return xc * lax.rsqrt(jnp.mean(xc * xc, axis=-1, keepdims=True) + LN_EPS)


def _sigmoid(x):
    return 1.0 / (1.0 + jnp.exp(-x))


def _silu(x):
    return x * _sigmoid(x)


def _dot(a, b):
    return jnp.dot(a.astype(BF16), b.astype(BF16), preferred_element_type=F32)


def _dot_nt(a, b):
    return lax.dot_general(a.astype(BF16), b.astype(BF16), NT_DIMS, preferred_element_type=F32)


def _split(a):
    hi = a.astype(BF16)
    lo = (a - hi.astype(F32)).astype(BF16)
    return hi, lo


def _dot3(a, b):
    ah, al = _split(a)
    bh, bl = _split(b)
    d = functools.partial(jnp.dot, preferred_element_type=F32)
    return d(ah, bh) + (d(ah, bl) + d(al, bh))


def _kth_threshold(count_ge, mn, mx, n_adm, topk):
    k = float(topk)
    all_in = n_adm <= k
    done0 = jnp.where(all_in | (count_ge(mx) >= k), 1.0, 0.0)
    thr0 = jnp.where(all_in, mn, mx)

    def cond(st):
        return (st[0] < MAX_SEARCH_ITERS) & (st[-1] > 0.0)

    def body(st):
        it, lo, hi, thr, done_f, _ = st
        done = done_f > 0.5
        mid = 0.5 * lo + 0.5 * hi
        stuck = jnp.logical_not((mid > lo) & (mid < hi))
        c = count_ge(mid)
        ge = c >= k
        hit = c == k
        thr = jnp.where(done, thr, jnp.where(hit, mid, jnp.where(stuck, lo, thr)))
        done_f = jnp.where(done | hit | stuck, 1.0, 0.0)
        return it + 1, jnp.where(ge, mid, lo), jnp.where(ge, hi, mid), thr, done_f, jnp.sum(1.0 - done_f)

    st = lax.while_loop(cond, body, (jnp.int32(0), mn, mx, thr0, done0, jnp.sum(1.0 - done0)))
    return jnp.where(st[4] > 0.5, st[3], st[1])


def _mod_kernel(c_ref, w_ref, b_ref, o_ref):
    o_ref[...] = _dot(_silu(c_ref[...]), w_ref[...]) + b_ref[...]


def _modulation(c, w_mod, b_mod):
    m, d = c.shape
    n = w_mod.shape[1]
    tn = 1024
    return pl.pallas_call(
        _mod_kernel,
        out_shape=jax.ShapeDtypeStruct((m, n), F32),
        grid=(n // tn,),
        in_specs=[pl.BlockSpec((m, d), lambda j: (0, 0)),
                  pl.BlockSpec((d, tn), lambda j: (0, j)),
                  pl.BlockSpec((1, tn), lambda j: (0, j))],
        out_specs=pl.BlockSpec((m, tn), lambda j: (0, j)),
        compiler_params=_cparams("arbitrary"),
    )(c, w_mod, b_mod.reshape(1, n))


def _rope_tables(pos, half):
    inv = ROPE_THETA ** (-jnp.arange(half, dtype=F32) / half)
    ang = pos.astype(F32)[:, None] * inv[None, :]
    c, s = jnp.cos(ang), jnp.sin(ang)
    rep = LANES // (2 * half)
    cos = jnp.tile(jnp.concatenate([c, c], -1), (1, rep))
    sin = jnp.tile(jnp.concatenate([-s, s], -1), (1, rep))
    return cos, sin


def _rope128(x, cos, sin):
    return x * cos + pltpu.roll(x, HEAD_DIM // 2, 1) * sin


def _rope64(x, cos, sin):
    lane = lax.broadcasted_iota(I32, x.shape, 1)
    first = (lane % IDX_DIM) < (IDX_DIM // 2)
    rot = jnp.where(first, pltpu.roll(x, LANES - IDX_DIM // 2, 1), pltpu.roll(x, IDX_DIM // 2, 1))
    return x * cos + rot * sin


def _inproj_kernel(x_ref, sh_ref, sc_ref, w_ref, c128_ref, s128_ref, c64_ref, s64_ref,
                   o_ref, u_ref, *, wi_scale):
    n = pl.program_id(1)

    @pl.when(n == 0)
    def _():
        y = _ln(x_ref[...]) * (1.0 + sc_ref[...]) + sh_ref[...]
        u_ref[...] = y.astype(BF16)

    acc = jnp.dot(u_ref[...], w_ref[...], preferred_element_type=F32)

    def kind(col):
        if col < QI_OFF or KA_OFF <= col < VA_OFF:
            return "rope128"
        if col < QB_OFF:
            return "rope64"
        return "small" if col == SM_OFF else "plain"

    def store(j, k):
        sl = slice(j * LANES, (j + 1) * LANES)
        a = acc[:, sl]
        if k == "rope128":
            a = _rope128(a, c128_ref[...], s128_ref[...])
        elif k == "rope64":
            a = _rope64(a, c64_ref[...], s64_ref[...])
        elif k == "small":
            lane = lax.broadcasted_iota(I32, a.shape, 1)
            r = _rope64(a, c64_ref[...], s64_ref[...])
            a = jnp.where(lane < SM_WI, r, jnp.where(lane < SM_BETA, a * wi_scale, a))
        o_ref[:, sl] = a

    sub = PACK_TN // LANES
    tile_kinds = [tuple(kind(t * PACK_TN + j * LANES) for j in range(sub)) for t in range(PACK_W // PACK_TN)]
    plain = ("plain",) * sub
    for kinds in sorted(set(tile_kinds) - {plain}):
        tiles = [t for t, k in enumerate(tile_kinds) if k == kinds]
        cond = functools.reduce(jnp.logical_or, [n == t for t in tiles])

        @pl.when(cond)
        def _(kinds=kinds):
            for j, k in enumerate(kinds):
                store(j, k)

    @pl.when(functools.reduce(jnp.logical_and, [n != t for t, k in enumerate(tile_kinds) if k != plain]))
    def _():
        o_ref[...] = acc


def _in_projection(x, sh, sc, w_pack, tabs, rows_per_group, tm):
    m, d = x.shape
    c128, s128, c64, s64 = tabs
    nt = PACK_W // PACK_TN
    if sh.ndim == 3:
        tpg = rows_per_group // tm
        mod_spec = pl.BlockSpec((None, 1, d), lambda i, n: (i // tpg, 0, 0))
    else:
        mod_spec = pl.BlockSpec((tm, d), lambda i, n: (i, 0))
    tab_blocks = c128.shape[0] // tm
    tab_spec = pl.BlockSpec((tm, LANES), lambda i, n: (i % tab_blocks, 0))
    wi_scale = float((IDX_HEADS * IDX_DIM) ** -0.5)
    return pl.pallas_call(
        functools.partial(_inproj_kernel, wi_scale=wi_scale),
        out_shape=jax.ShapeDtypeStruct((m, PACK_W), F32),
        grid=(m // tm, nt),
        in_specs=[pl.BlockSpec((tm, d), lambda i, n: (i, 0)), mod_spec, mod_spec,
                  pl.BlockSpec((d, PACK_TN), lambda i, n: (0, n)),
                  tab_spec, tab_spec, tab_spec, tab_spec],
        out_specs=pl.BlockSpec((tm, PACK_TN), lambda i, n: (i, n)),
        scratch_shapes=[pltpu.VMEM((tm, d), BF16)],
        compiler_params=_cparams("arbitrary", "arbitrary"),
    )(x, sh, sc, w_pack, c128, s128, c64, s64)


def _pack_w_in(w_in, d):
    heads = d // (2 * HEAD_DIM)
    conv_dim = 2 * heads * B_KEY_DIM + heads * B_VAL_DIM
    sizes = (heads * HEAD_DIM, A_KV_HEADS * HEAD_DIM, A_KV_HEADS * HEAD_DIM, IDX_HEADS * IDX_DIM,
             IDX_DIM, IDX_HEADS, conv_dim, heads * B_VAL_DIM, heads, heads)
    pts = [int(v) for v in np.cumsum(sizes)[:-1]]
    qa, ka, va, qi, ki, wi, qkv, z, braw, araw = jnp.split(w_in, pts, axis=1)
    small = jnp.concatenate([ki, wi, braw, araw], axis=1)
    small = jnp.pad(small, ((0, 0), (0, PACK_W - SM_OFF - small.shape[1])))
    w = jnp.concatenate([qa, qi, qkv, z, ka, va, small], axis=1)
    assert w.shape[1] == PACK_W
    return w.astype(BF16)


def _dsa_prompt_kernel(q_ref, qi_ref, sm_ref, k_ref, vt_ref, ki_ref, o_ref,
                       qt_ref, qit_ref, w_ref, sc_ref, s_ref, p_ref, m_ref, l_ref, a_ref, acc_ref,
                       b_ref, kmax_ref, lp_ref, *, tq, nq, topk, heads):
    i = pl.program_id(1)
    nkb = i + 1
    tk = tq
    scale = float(HEAD_DIM ** -0.5 * LOG2E)
    hpg = heads // A_KV_HEADS

    @pl.when(i == 0)
    def _():
        def body(kb, mxs):
            x = k_ref[pl.ds(pl.multiple_of(kb * tk, tk), tk), :].astype(F32)
            x2 = x * x
            return tuple(jnp.maximum(mxs[g], jnp.sum(x2[:, g * HEAD_DIM:(g + 1) * HEAD_DIM], axis=1, keepdims=True))
                         for g in range(A_KV_HEADS))
        mxs = lax.fori_loop(0, nq, body, (jnp.zeros((tk, 1), F32),) * A_KV_HEADS)
        for g in range(A_KV_HEADS):
            kmax_ref[g] = jnp.broadcast_to(jnp.sqrt(jnp.max(mxs[g], axis=0, keepdims=True)), kmax_ref.shape[1:])

    for h in range(heads):
        qt = jnp.transpose(q_ref[:, h * HEAD_DIM:(h + 1) * HEAD_DIM] * scale)
        qt_ref[h] = qt.astype(BF16)
        b_ref[h] = (jnp.sqrt(jnp.sum(qt * qt, axis=0, keepdims=True)) * kmax_ref[h // hpg, 0:1, :]) * BOUND_SLACK
    for hp in range(IDX_HEADS * IDX_DIM // LANES):
        t = jnp.transpose(qi_ref[:, hp * LANES:(hp + 1) * LANES])
        for u in range(LANES // IDX_DIM):
            qit_ref[hp * (LANES // IDX_DIM) + u] = t[u * IDX_DIM:(u + 1) * IDX_DIM].astype(BF16)
    w_ref[...] = jnp.transpose(sm_ref[...])[SM_WI:SM_WI + IDX_HEADS]
    qcol = i * tq + lax.broadcasted_iota(I32, (tk, tq), 1)
    krow = lax.broadcasted_iota(I32, (tk, tq), 0)

    def scores(kb, carry):
        mn, mx = carry
        off = pl.multiple_of(kb * tk, tk)
        kblk = ki_ref[pl.ds(off, tk), :]
        acc = jnp.zeros((tk, tq), F32)
        for h in range(IDX_HEADS):
            s = jnp.dot(kblk, qit_ref[h], preferred_element_type=F32)
            acc = acc + jnp.maximum(s, 0.0) * w_ref[h:h + 1, :]
        adm = krow + kb * tk <= qcol
        sc_ref[kb] = jnp.where(adm, acc, -jnp.inf)
        mn = jnp.minimum(mn, jnp.min(jnp.where(adm, acc, jnp.inf), axis=0, keepdims=True))
        mx = jnp.maximum(mx, jnp.max(jnp.where(adm, acc, -jnp.inf), axis=0, keepdims=True))
        return mn, mx

    mn, mx = lax.fori_loop(0, nkb, scores,
                           (jnp.full((1, tq), jnp.inf, F32), jnp.full((1, tq), -jnp.inf, F32)))

    sc_ref[nkb] = jnp.full((tk, tq), -jnp.inf, F32)

    def count_ge(t):
        def body(pi, cnt):
            for u in range(2):
                ind = jnp.where(sc_ref[2 * pi + u] >= t, 1.0, 0.0)
                cnt = cnt + jnp.sum(ind.reshape(tk // SUBLANES, SUBLANES, tq), axis=0)
            return cnt
        cnt = lax.fori_loop(0, (nkb + 1) // 2, body, jnp.zeros((SUBLANES, tq), F32))
        return jnp.sum(cnt, axis=0, keepdims=True)

    n_adm = (i * tq + lax.broadcasted_iota(I32, (1, tq), 1) + 1).astype(F32)
    thr = _kth_threshold(count_ge, mn, mx, n_adm, topk)
    acc_ref[...] = jnp.zeros(acc_ref.shape, F32)
    bmax = jnp.max(jnp.concatenate([b_ref[h] for h in range(heads)], axis=0))

    def keys_of(kb, h):
        g = h // hpg
        return k_ref[pl.ds(pl.multiple_of(kb * tk, tk), tk), g * HEAD_DIM:(g + 1) * HEAD_DIM]

    @pl.when(bmax <= EXP2_SAFE_BOUND)
    def _():
        lp_ref[...] = jnp.zeros(lp_ref.shape, F32)

        def attend(kb, c):
            bias = jnp.where(sc_ref[kb] >= thr, 0.0, MASK_BIAS)
            for h in range(heads):
                p = jnp.exp2(jnp.dot(keys_of(kb, h), qt_ref[h], preferred_element_type=F32) + (bias - b_ref[h]))
                lp_ref[h] += jnp.sum(p.reshape(tk // SUBLANES, SUBLANES, tq), axis=0)
                p_ref[h] = p.astype(BF16)
            for h in range(heads):
                acc_ref[h] += jnp.dot(vt_ref[kb, h // hpg], p_ref[h], preferred_element_type=F32)
            return c

        lax.fori_loop(0, nkb, attend, 0)
        for h in range(heads):
            l = jnp.sum(lp_ref[h], axis=0, keepdims=True)
            o_ref[:, h * HEAD_DIM:(h + 1) * HEAD_DIM] = jnp.transpose(acc_ref[h] / l).astype(BF16)

    @pl.when(bmax > EXP2_SAFE_BOUND)
    def _():
        m_ref[...] = jnp.full(m_ref.shape, NEG_BIG, F32)
        l_ref[...] = jnp.zeros(l_ref.shape, F32)

        def attend(kb, c):
            bias = jnp.where(sc_ref[kb] >= thr, 0.0, MASK_BIAS)
            for h in range(heads):
                s_ref[h] = jnp.dot(keys_of(kb, h), qt_ref[h], preferred_element_type=F32) + bias
            for h in range(heads):
                s = s_ref[h]
                m_old = m_ref[h]
                m_new = jnp.maximum(m_old, jnp.max(s, axis=0, keepdims=True))
                p = jnp.exp2(s - m_new)
                alpha = jnp.exp2(m_old - m_new)
                l_ref[h] = alpha * l_ref[h] + jnp.sum(p, axis=0, keepdims=True)
                p_ref[h] = p.astype(BF16)
                m_ref[h] = m_new
                a_ref[h] = alpha
            for h in range(heads):
                acc_ref[h] = a_ref[h] * acc_ref[h] + jnp.dot(vt_ref[kb, h // hpg], p_ref[h],
                                                             preferred_element_type=F32)
            return c

        lax.fori_loop(0, nkb, attend, 0)
        for h in range(heads):
            o_ref[:, h * HEAD_DIM:(h + 1) * HEAD_DIM] = jnp.transpose(acc_ref[h] / l_ref[h]).astype(BF16)


def _dsa_prompt(packed, k_bf, v_bf, ki_bf, batch, seq, heads):
    tq = min(DSA_TQ, seq)
    nq = seq // tq
    topk = min(TOPK_MAX, seq // 4)
    aw = heads * HEAD_DIM
    iw = IDX_HEADS * IDX_DIM
    kvw = A_KV_HEADS * HEAD_DIM
    vt = v_bf.reshape(batch, nq, tq, A_KV_HEADS, HEAD_DIM).transpose(0, 1, 3, 4, 2)
    return pl.pallas_call(
        functools.partial(_dsa_prompt_kernel, tq=tq, nq=nq, topk=topk, heads=heads),
        out_shape=jax.ShapeDtypeStruct((batch * seq, aw), BF16),
        grid=(batch, nq),
        in_specs=[pl.BlockSpec((tq, aw), lambda b, i: (b * nq + i, QA_OFF // aw)),
                  pl.BlockSpec((tq, iw), lambda b, i: (b * nq + i, QI_OFF // iw)),
                  pl.BlockSpec((tq, LANES), lambda b, i: (b * nq + i, SM_OFF // LANES)),
                  pl.BlockSpec((seq, kvw), lambda b, i: (b, 0)),
                  pl.BlockSpec((None, nq, A_KV_HEADS, HEAD_DIM, tq), lambda b, i: (b, 0, 0, 0, 0)),
                  pl.BlockSpec((seq, IDX_DIM), lambda b, i: (b, 0))],
        out_specs=pl.BlockSpec((tq, aw), lambda b, i: (b * nq + i, 0)),
        scratch_shapes=[pltpu.VMEM((heads, HEAD_DIM, tq), BF16),
                        pltpu.VMEM((IDX_HEADS, IDX_DIM, tq), BF16),
                        pltpu.VMEM((IDX_HEADS, tq), F32),
                        pltpu.VMEM((nq + 1, tq, tq), F32),
                        pltpu.VMEM((heads, tq, tq), F32),
                        pltpu.VMEM((heads, tq, tq), BF16),
                        pltpu.VMEM((heads, 1, tq), F32),
                        pltpu.VMEM((heads, 1, tq), F32),
                        pltpu.VMEM((heads, 1, tq), F32),
                        pltpu.VMEM((heads, HEAD_DIM, tq), F32),
                        pltpu.VMEM((heads, 1, tq), F32),
                        pltpu.VMEM((A_KV_HEADS, SUBLANES, tq), F32),
                        pltpu.VMEM((heads, SUBLANES, tq), F32)],
        compiler_params=_cparams("arbitrary", "arbitrary"),
    )(packed, packed, packed, k_bf, vt, ki_bf)


def _dsa_sample_kernel(pt_ref, q_ref, qi_ref, sm_ref, kn_ref, vn_ref, *rest,
                       pg, npg, past_len, t_real, topk, heads):
    del pt_ref
    kidx_refs = rest[:pg]
    k_refs = rest[pg:2 * pg]
    v_refs = rest[2 * pg:3 * pg]
    o_ref = rest[3 * pg]
    qi2_ref, w_ref, qt_ref, sc_ref, mm_ref, thr_ref, pad_ref, m_ref, l_ref, acc_ref = rest[3 * pg + 1:]
    j = pl.program_id(1)
    rows = SUBLANES
    ps = LANES
    scale = float(HEAD_DIM ** -0.5)
    hpg = heads // A_KV_HEADS

    def indexer(kt):
        n = kt.shape[1]
        s = jnp.dot(qi2_ref[...], kt.astype(BF16), preferred_element_type=F32)
        x = jnp.maximum(s, 0.0) * jnp.tile(w_ref[...], (1, n // LANES))
        return jnp.sum(x.reshape(IDX_HEADS, rows, n), axis=0)

    def lane_form(x):
        return jnp.transpose(jnp.tile(x, (LANES // rows, 1)))

    def fold_lanes(x, op):
        r = x[:, 0:LANES]
        for t in range(1, x.shape[1] // LANES):
            r = op(r, x[:, t * LANES:(t + 1) * LANES])
        return r

    def padded(block):
        pad_ref[...] = jnp.zeros(pad_ref.shape, F32)
        pad_ref[0:rows, 0:block.shape[1]] = block
        return pad_ref[:, 0:block.shape[1]]

    def softmax_step(s, v_of_group):
        m_old = m_ref[0:1, :]
        m_new = jnp.maximum(m_old, jnp.max(s, axis=0, keepdims=True))
        p = jnp.exp(s - m_new).astype(BF16)
        alpha = jnp.exp(m_old - m_new)
        l_ref[...] = jnp.broadcast_to(alpha * l_ref[0:1, :] + jnp.sum(p.astype(F32), axis=0, keepdims=True),
                                      l_ref.shape)
        m_ref[...] = jnp.broadcast_to(m_new, m_ref.shape)
        for g in range(A_KV_HEADS):
            acc_ref[g] = alpha * acc_ref[g] + lax.dot_general(
                v_of_group(g).astype(BF16), p, TN_DIMS, preferred_element_type=F32)

    @pl.when(j == 0)
    def _():
        for h in range(IDX_HEADS):
            qi2_ref[h * rows:(h + 1) * rows, :] = qi_ref[:, h * IDX_DIM:(h + 1) * IDX_DIM].astype(BF16)
            w_ref[h * rows:(h + 1) * rows, :] = jnp.broadcast_to(
                sm_ref[:, SM_WI + h:SM_WI + h + 1], (rows, LANES))
        for g in range(A_KV_HEADS):
            pad_ref[...] = jnp.zeros(pad_ref.shape, F32)
            for h in range(g * hpg, (g + 1) * hpg):
                pad_ref[h * rows:(h + 1) * rows, :] = q_ref[:, h * HEAD_DIM:(h + 1) * HEAD_DIM] * scale
            qt_ref[g] = jnp.transpose(pad_ref[...]).astype(BF16)
        mm_ref[0] = jnp.full((rows, LANES), jnp.inf, F32)
        mm_ref[1] = jnp.full((rows, LANES), -jnp.inf, F32)
        m_ref[...] = jnp.full(m_ref.shape, NEG_BIG, F32)
        l_ref[...] = jnp.zeros(l_ref.shape, F32)
        acc_ref[...] = jnp.zeros(acc_ref.shape, F32)

    @pl.when(j < npg)
    def _():
        x = indexer(jnp.concatenate([r[...] for r in kidx_refs], axis=1))
        sc_ref[j] = x
        mm_ref[0] = jnp.minimum(mm_ref[0], fold_lanes(x, jnp.minimum))
        mm_ref[1] = jnp.maximum(mm_ref[1], fold_lanes(x, jnp.maximum))

    @pl.when(j == npg - 1)
    def _():
        kt_new = jnp.transpose(padded(sm_ref[...]))[SM_KI:SM_KI + IDX_DIM]
        x = indexer(kt_new)
        qrow = lax.broadcasted_iota(I32, (rows, ps), 0)
        kcol = lax.broadcasted_iota(I32, (rows, ps), 1)
        adm = (kcol <= qrow) & (kcol < t_real)
        sc_ref[npg] = jnp.full(sc_ref.shape[1:], -jnp.inf, F32)
        sc_ref[npg, :, 0:ps] = jnp.where(adm, x, -jnp.inf)
        mn = jnp.min(jnp.minimum(mm_ref[0], jnp.where(adm, x, jnp.inf)), axis=1, keepdims=True)
        mx = jnp.max(jnp.maximum(mm_ref[1], jnp.where(adm, x, -jnp.inf)), axis=1, keepdims=True)

        def count_ge(t):
            def body(b, cnt):
                return cnt + fold_lanes(jnp.where(sc_ref[b] >= t, 1.0, 0.0), jnp.add)
            cnt = lax.fori_loop(0, npg + 1, body, jnp.zeros((rows, LANES), F32))
            return jnp.sum(cnt, axis=1, keepdims=True)

        q1 = lax.broadcasted_iota(I32, (rows, 1), 0)
        n_adm = (past_len + jnp.minimum(q1, t_real - 1) + 1).astype(F32)
        thr = _kth_threshold(count_ge, mn, mx, n_adm, topk)
        thr_ref[...] = lane_form(jnp.broadcast_to(thr, (rows, LANES)))[0:rows]

    @pl.when(j >= npg)
    def _():
        bias = jnp.where(lane_form(sc_ref[j - npg]) >= thr_ref[0:1, :], 0.0, MASK_BIAS)
        s = bias
        for g in range(A_KV_HEADS):
            kg = jnp.concatenate([r[pl.ds(g, ps, stride=A_KV_HEADS), :] for r in k_refs], axis=0)
            s = s + jnp.dot(kg.astype(BF16), qt_ref[g], preferred_element_type=F32)
        softmax_step(s, lambda g: jnp.concatenate([r[pl.ds(g, ps, stride=A_KV_HEADS), :] for r in v_refs], axis=0))

    @pl.when(j == 2 * npg - 1)
    def _():
        s = jnp.where(lane_form(sc_ref[npg, :, 0:ps]) >= thr_ref[0:1, :], 0.0, MASK_BIAS)
        for g in range(A_KV_HEADS):
            kg = padded(kn_ref[:, g * HEAD_DIM:(g + 1) * HEAD_DIM])
            s = s + jnp.dot(kg.astype(BF16), qt_ref[g], preferred_element_type=F32)
        softmax_step(s, lambda g: padded(vn_ref[:, g * HEAD_DIM:(g + 1) * HEAD_DIM]))
        for g in range(A_KV_HEADS):
            out = jnp.transpose(acc_ref[g] / l_ref[0:1, :])
            for h in range(g * hpg, (g + 1) * hpg):
                o_ref[:, h * HEAD_DIM:(h + 1) * HEAD_DIM] = out[h * rows:(h + 1) * rows, :].astype(BF16)


def _dsa_sample(packed3, cache_k, cache_v, cache_kidx, page_table, t_real, heads):
    nb, rows, _ = packed3.shape
    assert rows == SUBLANES and IDX_HEADS * rows == LANES
    ps = cache_k.shape[2]
    assert ps == LANES and cache_k.shape[0] == 1
    n_pages = page_table.shape[1]
    pg = min(SAMPLE_PAGES_PER_STEP, n_pages)
    npg = n_pages // pg
    assert n_pages % pg == 0
    kvw = A_KV_HEADS * HEAD_DIM
    aw = heads * HEAD_DIM
    iw = IDX_HEADS * IDX_DIM
    past_len = n_pages * ps
    topk = min(TOPK_MAX, (past_len + t_real) // 4)
    pt = page_table.reshape(-1).astype(I32)

    def page1(b, j, pt, i):
        return pt[b * n_pages + jnp.minimum(j, npg - 1) * pg + i]

    def page2(b, j, pt, i):
        return pt[b * n_pages + jnp.maximum(j - npg, 0) * pg + i]

    kidx_t = jnp.swapaxes(cache_kidx, 2, 3)

    def kidx_spec(i):
        return pl.BlockSpec((None, None, IDX_DIM, ps), lambda b, j, pt: (0, page1(b, j, pt, i), 0, 0))

    ck = cache_k.reshape(cache_k.shape[1], ps * A_KV_HEADS, HEAD_DIM)
    cv = cache_v.reshape(cache_v.shape[1], ps * A_KV_HEADS, HEAD_DIM)

    def kv_spec(i):
        return pl.BlockSpec((None, ps * A_KV_HEADS, HEAD_DIM),
                            lambda b, j, pt: (page2(b, j, pt, i), 0, 0))

    in_specs = [pl.BlockSpec((None, rows, aw), lambda b, j, pt: (b, 0, QA_OFF // aw)),
                pl.BlockSpec((None, rows, iw), lambda b, j, pt: (b, 0, QI_OFF // iw)),
                pl.BlockSpec((None, rows, LANES), lambda b, j, pt: (b, 0, SM_OFF // LANES)),
                pl.BlockSpec((None, rows, kvw), lambda b, j, pt: (b, 0, KA_OFF // kvw)),
                pl.BlockSpec((None, rows, kvw), lambda b, j, pt: (b, 0, VA_OFF // kvw))]
    in_specs += [kidx_spec(i) for i in range(pg)]
    in_specs += [kv_spec(i) for i in range(pg)]
    in_specs += [kv_spec(i) for i in range(pg)]
    return pl.pallas_call(
        functools.partial(_dsa_sample_kernel, pg=pg, npg=npg, past_len=past_len, t_real=t_real,
                          topk=topk, heads=heads),
        out_shape=jax.ShapeDtypeStruct((nb, rows, aw), BF16),
        grid_spec=pltpu.PrefetchScalarGridSpec(
            num_scalar_prefetch=1,
            grid=(nb, 2 * npg),
            in_specs=in_specs,
            out_specs=pl.BlockSpec((None, rows, aw), lambda b, j, pt: (b, 0, 0)),
            scratch_shapes=[pltpu.VMEM((IDX_HEADS * rows, IDX_DIM), BF16),
                            pltpu.VMEM((IDX_HEADS * rows, LANES), F32),
                            pltpu.VMEM((A_KV_HEADS, HEAD_DIM, LANES), BF16),
                            pltpu.VMEM((npg + 1, rows, pg * ps), F32),
                            pltpu.VMEM((2, rows, LANES), F32),
                            pltpu.VMEM((rows, LANES), F32),
                            pltpu.VMEM((LANES, LANES), F32),
                            pltpu.VMEM((rows, LANES), F32),
                            pltpu.VMEM((rows, LANES), F32),
                            pltpu.VMEM((A_KV_HEADS, HEAD_DIM, LANES), F32)]),
        compiler_params=_cparams("arbitrary", "arbitrary"),
    )(pt, packed3, packed3, packed3, packed3, packed3,
      *([kidx_t] * pg), *([ck] * pg), *([cv] * pg))


def _split3(x):
    x1 = x.astype(BF16)
    r = x - x1.astype(F32)
    x2 = r.astype(BF16)
    return x1, x2, (r - x2.astype(F32)).astype(BF16)


def _exact_right(x, m01):
    d = functools.partial(jnp.dot, preferred_element_type=F32)
    x1, x2, x3 = _split3(x)
    return (d(x1, m01) + d(x2, m01)) + d(x3, m01)


def _exact_left(m01, x):
    d = functools.partial(jnp.dot, preferred_element_type=F32)
    x1, x2, x3 = _split3(x)
    return (d(m01, x1) + d(m01, x2)) + d(m01, x3)


def _dot3s(ah, al, bh, bl):
    d = functools.partial(jnp.dot, preferred_element_type=F32)
    return d(ah, bh) + (d(ah, bl) + d(al, bh))


def _blockdiag_mask(nb, rb, cb):
    shape = (nb * rb, nb * cb)
    return lax.broadcasted_iota(I32, shape, 0) // rb == lax.broadcasted_iota(I32, shape, 1) // cb


def _blockdiag(x, mask):
    t = jnp.tile(x, (mask.shape[0] // x.shape[0], 1))
    return jnp.where(mask, t, jnp.zeros_like(t))


def _unit_lower_inverse_cat(a_cats, c, mask):
    shape = a_cats[0].shape
    eye = (lax.broadcasted_iota(I32, shape, 0) == lax.broadcasted_iota(I32, shape, 1) % c).astype(F32)

    def bd_halves(p):
        ph, pl_ = _split(p)
        return ph, pl_, _blockdiag(ph, mask), _blockdiag(pl_, mask)

    xs = [eye - a for a in a_cats]
    ps = []
    for a in a_cats:
        ph, pl_, bh, bl = bd_halves(a)
        ps.append(_dot3s(ph, pl_, bh, bl))
    n = 2
    while n < c:
        nxt_x, nxt_p = [], []
        for x, p in zip(xs, ps):
            _, _, bh, bl = bd_halves(p)
            if 2 * n < c:
                lh, ll = _split(jnp.concatenate([x, p], axis=0))
                r = _dot3s(lh, ll, bh, bl)
                nxt_x.append(x + r[0:c])
                nxt_p.append(r[c:2 * c])
            else:
                xh, xl = _split(x)
                nxt_x.append(x + _dot3s(xh, xl, bh, bl))
                nxt_p.append(p)
        xs, ps = nxt_x, nxt_p
        n *= 2
    d = functools.partial(jnp.dot, preferred_element_type=F32)
    res = []
    for a, x in zip(a_cats, xs):
        t1, t2, t3 = _split3(eye + a)
        x1, x2, x3 = (_blockdiag(v, mask) for v in _split3(x))
        tx = ((d(t1, x1) + d(t1, x2)) + (d(t2, x1) + d(t1, x3))) + (d(t2, x2) + d(t3, x1))
        res.append(eye - tx)
    out = []
    for x, r in zip(xs, res):
        xh, xl = _split(x)
        rh, rl = _split(r)
        out.append(x + _dot3s(xh, xl, _blockdiag(rh, mask), _blockdiag(rl, mask)))
    return out


def _delta_kernel(qb_ref, kb_ref, vb_ref, z_ref, sm_ref, conv0_ref, wconv_ref, alog_ref, dtb_ref,
                  wnorm_ref, ssm0_ref, eb_ref, eg_ref, egc_ref, o_ref, convo_ref, ssmo_ref, xbuf_ref, s_ref,
                  *, tci, tcc, t_real, heads):
    j = pl.program_id(1)
    c = DELTA_CHUNK
    hw = heads * B_KEY_DIM
    halo = SUBLANES
    hg = DELTA_GROUP
    gw, cw, pw = hg * B_KEY_DIM, hg * c, 2 * B_KEY_DIM

    @pl.when(j == 0)
    def _():
        xbuf_ref[0:halo, :] = conv0_ref[...]
        s_ref[...] = jnp.zeros(s_ref.shape, F32)
        for h in range(heads):
            o = (h % 2) * B_KEY_DIM
            s_ref[h // 2, o:o + B_KEY_DIM, o:o + B_VAL_DIM] = ssm0_ref[h]

    @pl.when(j > 0)
    def _():
        xbuf_ref[0:halo, :] = xbuf_ref[tcc:tcc + halo, :]

    xbuf_ref[halo:halo + tci, 0:hw] = qb_ref[...]
    xbuf_ref[halo:halo + tci, hw:2 * hw] = kb_ref[...]
    xbuf_ref[halo:halo + tci, 2 * hw:3 * hw] = vb_ref[...]
    if tci < tcc:
        xbuf_ref[halo + tci:halo + tcc, :] = jnp.zeros((tcc - tci, 3 * hw), F32)

    jr, r1 = (t_real - 1) // tcc, t_real - ((t_real - 1) // tcc) * tcc

    @pl.when(j == jr)
    def _():
        convo_ref[...] = jnp.zeros(convo_ref.shape, F32)
        convo_ref[halo - (CONV_WIDTH - 1):halo, :] = xbuf_ref[halo + r1 - (CONV_WIDTH - 1):halo + r1, :]

    conv = xbuf_ref[halo - 3:halo - 3 + tcc, :] * wconv_ref[0:1, :]
    for t in range(1, CONV_WIDTH):
        conv = conv + xbuf_ref[halo - 3 + t:halo - 3 + t + tcc, :] * wconv_ref[t:t + 1, :]
    act = _silu(conv)

    sm = sm_ref[...]
    if tci < tcc:
        sm = jnp.concatenate([sm, jnp.zeros((tcc - tci, LANES), F32)], axis=0)
    lane = lax.broadcasted_iota(I32, (tcc, LANES), 1)
    trow = j * tcc + lax.broadcasted_iota(I32, (tcc, LANES), 0)
    valid = trow < t_real
    beta_t = jnp.where(valid & (lane >= SM_BETA) & (lane < SM_BETA + heads), _sigmoid(sm), 0.0)
    xg = sm + dtb_ref[...]
    softplus = jnp.maximum(xg, 0.0) + jnp.log1p(jnp.exp(-jnp.abs(xg)))
    lg_t = jnp.where(valid & (lane >= SM_DECAY) & (lane < SM_DECAY + heads),
                     -jnp.exp(alog_ref[...]) * softplus, 0.0)
    ri = lax.broadcasted_iota(I32, (tcc, tcc), 0)
    ci = lax.broadcasted_iota(I32, (tcc, tcc), 1)
    tri = ((ci <= ri) & (ci // c == ri // c)).astype(BF16)
    g_t = _exact_left(tri, lg_t)
    rmask = valid[:, 0:1]

    def l2norm_heads(x, scale):
        parts = []
        for h in range(heads):
            xh = x[:, h * B_KEY_DIM:(h + 1) * B_KEY_DIM]
            parts.append(xh * (lax.rsqrt(jnp.sum(xh * xh, axis=-1, keepdims=True) + L2_EPS) * scale))
        return jnp.concatenate(parts, axis=1)

    qn_all = l2norm_heads(act[:, 0:hw], float(B_KEY_DIM ** -0.5))
    kn_all = l2norm_heads(act[:, hw:2 * hw], 1.0)
    v_all = act[:, 2 * hw:3 * hw]
    if t_real % tcc != 0 or tci < tcc:
        kn_all = jnp.where(rmask, kn_all, 0.0)
        v_all = jnp.where(rmask, v_all, 0.0)
    beta_e = _exact_right(beta_t, eb_ref[...])
    g_e = _exact_right(g_t, eg_ref[...])
    gc_all = _exact_right(g_t, egc_ref[...])
    eg_e = jnp.exp(g_e)
    kbeta_all = kn_all * beta_e
    vbeta_all = v_all * beta_e
    kbe_all = kbeta_all * eg_e
    qe_all = qn_all * eg_e

    crow = lax.broadcasted_iota(I32, (c, heads * c), 0)
    ccol = lax.broadcasted_iota(I32, (c, heads * c), 1) % c
    strict = (lax.broadcasted_iota(I32, (c, cw), 0) > lax.broadcasted_iota(I32, (c, cw), 1) % c)
    ones_cc = jnp.ones((c, c), BF16)
    assert B_KEY_DIM == B_VAL_DIM
    mask_c = _blockdiag_mask(hg, c, c)
    mask_d = _blockdiag_mask(hg, c, B_KEY_DIM)
    prow = lax.broadcasted_iota(I32, (pw, pw), 0) // B_KEY_DIM
    pcol = lax.broadcasted_iota(I32, (pw, pw), 1) // B_VAL_DIM
    units = [(ch, grp) for ch in range(tcc // c) for grp in range(heads // hg)]
    a_cats, qk_cats = [], []
    for ch in range(tcc // c):
        r = slice(ch * c, (ch + 1) * c)
        gc = gc_all[r]
        grow = _exact_left(ones_cc, jnp.where(crow == ccol, gc, 0.0))
        decay = jnp.where(crow >= ccol, jnp.exp(jnp.minimum(gc - grow, 0.0)), 0.0)
        for grp in range(heads // hg):
            ls = slice(grp * gw, (grp + 1) * gw)
            dec = decay[:, grp * cw:(grp + 1) * cw]
            kn_bd = _blockdiag(kn_all[r, ls].astype(BF16), mask_d)
            lhs = jnp.concatenate([kbeta_all[r, ls], qn_all[r, ls]], axis=0).astype(BF16)
            aq = lax.dot_general(lhs, kn_bd, NT_DIMS, preferred_element_type=F32)
            a_cats.append(jnp.where(strict, aq[0:c] * dec, 0.0))
            qk_cats.append(aq[c:2 * c] * dec)
    tinvs = _unit_lower_inverse_cat(a_cats, c, mask_c)
    us, ws = [], []
    for (ch, grp), tinv in zip(units, tinvs):
        r = slice(ch * c, (ch + 1) * c)
        ls = slice(grp * gw, (grp + 1) * gw)
        th, tl = _split(tinv)
        vh_, vl_ = _split(vbeta_all[r, ls])
        kh_, kl_ = _split(kbe_all[r, ls])
        us.append(_dot3s(th, tl, _blockdiag(vh_, mask_d), _blockdiag(vl_, mask_d)))
        ws.append(_dot3s(th, tl, _blockdiag(kh_, mask_d), _blockdiag(kl_, mask_d)))

    for ch in range(tcc // c):
        r = slice(ch * c, (ch + 1) * c)
        glast = g_e[(ch + 1) * c - 1:(ch + 1) * c, :]
        kdec_all = kn_all[r] * jnp.exp(glast - g_e[r])
        eglast = jnp.exp(glast)
        for grp in range(heads // hg):
            un = units.index((ch, grp))
            u, w, qk_cat = us[un], ws[un], qk_cats[un]
            vnew, qs = [], []
            for pr in range(hg // 2):
                idx = grp * (hg // 2) + pr
                la = slice(grp * gw + pr * pw, grp * gw + (pr + 1) * pw)
                lg_ = slice(pr * pw, (pr + 1) * pw)
                s_old = s_ref[idx]
                lhs2 = jnp.concatenate([w[:, lg_], qe_all[r, la]], axis=0).astype(BF16)
                rs = jnp.dot(lhs2, s_old.astype(BF16), preferred_element_type=F32)
                vn = u[:, lg_] - rs[0:c]
                upd = lax.dot_general(kdec_all[:, la].astype(BF16), vn.astype(BF16), TN_DIMS,
                                      preferred_element_type=F32)
                s_ref[idx] = s_old * eglast[:, la] + jnp.where(prow == pcol, upd, 0.0)
                vnew.append(vn)
                qs.append(rs[c:2 * c])
            vn_bd = _blockdiag(jnp.concatenate(vnew, axis=1).astype(BF16), mask_d)
            o_g = jnp.concatenate(qs, axis=1) + jnp.dot(qk_cat.astype(BF16), vn_bd, preferred_element_type=F32)
            if ch * c < tci:
                ro = slice(ch * c, min((ch + 1) * c, tci))
                nr = ro.stop - ro.start
                for a in range(hg):
                    h = grp * hg + a
                    oh = o_g[0:nr, a * B_VAL_DIM:(a + 1) * B_VAL_DIM]
                    zz = z_ref[ro, h * B_VAL_DIM:(h + 1) * B_VAL_DIM]
                    on = oh * lax.rsqrt(jnp.mean(oh * oh, axis=-1, keepdims=True) + RMS_EPS)
                    o_ref[ro, h * B_VAL_DIM:(h + 1) * B_VAL_DIM] = (on * wnorm_ref[...] * _silu(zz)).astype(BF16)

    @pl.when(j == pl.num_programs(1) - 1)
    def _():
        for h in range(heads):
            o = (h % 2) * B_KEY_DIM
            ssmo_ref[h] = s_ref[h // 2, o:o + B_KEY_DIM, o:o + B_VAL_DIM]


def _delta_net(packed, conv0, ssm0, w_conv, a_log, dt_bias, w_onorm, batch, t_pad, t_real, tci, tcc, heads):
    hw = heads * B_KEY_DIM
    nt = t_pad // tci
    assert t_pad % tci == 0 and tcc % DELTA_CHUNK == 0 and tci % SUBLANES == 0
    assert (nt == 1 and tci <= tcc) or tci == tcc
    assert heads % DELTA_GROUP == 0 and DELTA_GROUP % 2 == 0
    wconv_p = jnp.pad(w_conv, ((0, SUBLANES - CONV_WIDTH), (0, 0)))
    alog_row = jnp.zeros((1, LANES), F32).at[0, SM_DECAY:SM_DECAY + heads].set(a_log)
    dtb_row = jnp.zeros((1, LANES), F32).at[0, SM_DECAY:SM_DECAY + heads].set(dt_bias)
    def spread(src, width):
        m = np.zeros((LANES, heads * width), np.float32)
        for h in range(heads):
            m[src + h, h * width:(h + 1) * width] = 1.0
        return jnp.asarray(m, BF16)
    e_beta, e_g, e_gc = spread(SM_BETA, B_KEY_DIM), spread(SM_DECAY, B_KEY_DIM), spread(SM_DECAY, DELTA_CHUNK)
    full2 = lambda a: pl.BlockSpec(a.shape, lambda b, j: (0, 0))
    row = lambda b, j: (b * nt + j)
    return pl.pallas_call(
        functools.partial(_delta_kernel, tci=tci, tcc=tcc, t_real=t_real, heads=heads),
        out_shape=(jax.ShapeDtypeStruct((batch * t_pad, hw), BF16),
                   jax.ShapeDtypeStruct((batch, SUBLANES, 3 * hw), F32),
                   jax.ShapeDtypeStruct((batch, heads, B_KEY_DIM, B_VAL_DIM), F32)),
        grid=(batch, nt),
        in_specs=[pl.BlockSpec((tci, hw), lambda b, j: (row(b, j), QB_OFF // hw)),
                  pl.BlockSpec((tci, hw), lambda b, j: (row(b, j), KB_OFF // hw)),
                  pl.BlockSpec((tci, hw), lambda b, j: (row(b, j), VB_OFF // hw)),
                  pl.BlockSpec((tci, hw), lambda b, j: (row(b, j), Z_OFF // hw)),
                  pl.BlockSpec((tci, LANES), lambda b, j: (row(b, j), SM_OFF // LANES)),
                  pl.BlockSpec((None, SUBLANES, 3 * hw), lambda b, j: (b, 0, 0)),
                  pl.BlockSpec((SUBLANES, 3 * hw), lambda b, j: (0, 0)),
                  pl.BlockSpec((1, LANES), lambda b, j: (0, 0)),
                  pl.BlockSpec((1, LANES), lambda b, j: (0, 0)),
                  pl.BlockSpec((1, B_VAL_DIM), lambda b, j: (0, 0)),
                  pl.BlockSpec((None, heads, B_KEY_DIM, B_VAL_DIM), lambda b, j: (b, 0, 0, 0)),
                  full2(e_beta), full2(e_g), full2(e_gc)],
        out_specs=(pl.BlockSpec((tci, hw), lambda b, j: (row(b, j), 0)),
                   pl.BlockSpec((None, SUBLANES, 3 * hw), lambda b, j: (b, 0, 0)),
                   pl.BlockSpec((None, heads, B_KEY_DIM, B_VAL_DIM), lambda b, j: (b, 0, 0, 0))),
        scratch_shapes=[pltpu.VMEM((tcc + SUBLANES, 3 * hw), F32),
                        pltpu.VMEM((heads // 2, 2 * B_KEY_DIM, 2 * B_VAL_DIM), F32)],
        compiler_params=_cparams("arbitrary", "arbitrary"),
    )(packed, packed, packed, packed, packed, conv0, wconv_p, alog_row, dtb_row,
      w_onorm.reshape(1, B_VAL_DIM), ssm0, e_beta, e_g, e_gc)


def _outproj_kernel(oa_ref, ob_ref, x_ref, wa_ref, wb_ref, gt_ref, sh_ref, sc_ref, g_ref, b_ref,
                    wr_ref, br_ref, x1_ref, u2_ref, lg_ref, *, alpha):
    mix = (jnp.dot(oa_ref[...], wa_ref[...], preferred_element_type=F32)
           + jnp.dot(ob_ref[...], wb_ref[...], preferred_element_type=F32))
    x1 = _ln(alpha * x_ref[...] + gt_ref[...] * mix) * g_ref[...] + b_ref[...]
    x1_ref[...] = x1
    u2 = _ln(x1) * (1.0 + sc_ref[...]) + sh_ref[...]
    u2_ref[...] = u2.astype(BF16)
    lg_ref[...] = _dot3(u2, wr_ref[...]) + br_ref[...]


def _out_projection(oa, ob, x, wa, wb, gt, sh, sc, ln_g, ln_b, w_r, b_r, rows_per_group, tm, alpha):
    m, d = x.shape
    hwid = oa.shape[1]
    if gt.ndim == 3:
        tpg = rows_per_group // tm
        mod_spec = pl.BlockSpec((None, 1, d), lambda i: (i // tpg, 0, 0))
    else:
        mod_spec = pl.BlockSpec((tm, d), lambda i: (i, 0))
    full = lambda r, c: pl.BlockSpec((r, c), lambda i: (0, 0))
    return pl.pallas_call(
        functools.partial(_outproj_kernel, alpha=alpha),
        out_shape=(jax.ShapeDtypeStruct((m, d), F32), jax.ShapeDtypeStruct((m, d), BF16),
                   jax.ShapeDtypeStruct((m, LANES), F32)),
        grid=(m // tm,),
        in_specs=[pl.BlockSpec((tm, hwid), lambda i: (i, 0)), pl.BlockSpec((tm, hwid), lambda i: (i, 0)),
                  pl.BlockSpec((tm, d), lambda i: (i, 0)), full(hwid, d), full(hwid, d),
                  mod_spec, mod_spec, mod_spec, full(1, d), full(1, d), full(d, LANES), full(1, LANES)],
        out_specs=(pl.BlockSpec((tm, d), lambda i: (i, 0)), pl.BlockSpec((tm, d), lambda i: (i, 0)),
                   pl.BlockSpec((tm, LANES), lambda i: (i, 0))),
        compiler_params=_cparams("arbitrary"),
    )(oa, ob, x, wa, wb, gt, sh, sc, ln_g.reshape(1, d), ln_b.reshape(1, d), w_r, b_r)


def _combine_weights(lg):
    lane = lax.broadcasted_iota(I32, lg.shape, 1)
    big = jnp.int32(LANES)
    gmask = lane < N_GROUPS
    gl = jnp.where(gmask, lg, -jnp.inf)
    gmax = jnp.max(gl, axis=-1, keepdims=True)
    g_idx = jnp.min(jnp.where(gl == gmax, lane, big), axis=-1, keepdims=True)
    p_top = 1.0 / jnp.sum(jnp.where(gmask, jnp.exp(gl - gmax), 0.0), axis=-1, keepdims=True)
    e_lane = lane - RT_EXP
    in_grp = (e_lane >= 0) & (e_lane < N_EXPERTS) & ((e_lane // EXPERTS_PER_GROUP) == g_idx)
    v = jnp.where(in_grp, lg, -jnp.inf)
    v1 = jnp.max(v, axis=-1, keepdims=True)
    i1 = jnp.min(jnp.where(v == v1, lane, big), axis=-1, keepdims=True)
    vr = jnp.where(lane == i1, -jnp.inf, v)
    v2 = jnp.max(vr, axis=-1, keepdims=True)
    i2 = jnp.min(jnp.where(vr == v2, lane, big), axis=-1, keepdims=True)
    e2 = jnp.exp(v2 - v1)
    den = 1.0 / (1.0 + e2)
    return jnp.where(lane == i1, den * p_top, jnp.where(lane == i2, e2 * den * p_top, 0.0))


def _moe_kernel(u_ref, lg_ref, x1_ref, wg_ref, wu_ref, wd_ref, gt_ref, g_ref, b_ref, o_ref,
                acc_ref, cmb_ref, *, alpha):
    e = pl.program_id(1)

    @pl.when(e == 0)
    def _():
        cmb_ref[...] = _combine_weights(lg_ref[...])
        acc_ref[...] = jnp.zeros(acc_ref.shape, F32)

    u = u_ref[...]
    h = jnp.dot(u, wg_ref[...], preferred_element_type=F32)
    up = jnp.dot(u, wu_ref[...], preferred_element_type=F32)
    lane = lax.broadcasted_iota(I32, cmb_ref.shape, 1)
    c_e = jnp.sum(jnp.where(lane == e + RT_EXP, cmb_ref[...], 0.0), axis=-1, keepdims=True)
    act = _silu(h) * up * c_e
    acc_ref[...] += jnp.dot(act.astype(BF16), wd_ref[...], preferred_element_type=F32)

    @pl.when(e == pl.num_programs(1) - 1)
    def _():
        o_ref[...] = _ln(alpha * x1_ref[...] + gt_ref[...] * acc_ref[...]) * g_ref[...] + b_ref[...]


def _moe(u2, logits, x1, wg, wu, wd, gt, ln_g, ln_b, rows_per_group, tm, alpha):
    m, d = x1.shape
    ne, _, de = wg.shape
    if gt.ndim == 3:
        tpg = rows_per_group // tm
        mod_spec = pl.BlockSpec((None, 1, d), lambda i, e: (i // tpg, 0, 0))
    else:
        mod_spec = pl.BlockSpec((tm, d), lambda i, e: (i, 0))
    return pl.pallas_call(
        functools.partial(_moe_kernel, alpha=alpha),
        out_shape=jax.ShapeDtypeStruct((m, d), F32),
        grid=(m // tm, ne),
        in_specs=[pl.BlockSpec((tm, d), lambda i, e: (i, 0)),
                  pl.BlockSpec((tm, LANES), lambda i, e: (i, 0)),
                  pl.BlockSpec((tm, d), lambda i, e: (i, 0)),
                  pl.BlockSpec((None, d, de), lambda i, e: (e, 0, 0)),
                  pl.BlockSpec((None, d, de), lambda i, e: (e, 0, 0)),
                  pl.BlockSpec((None, de, d), lambda i, e: (e, 0, 0)),
                  mod_spec,
                  pl.BlockSpec((1, d), lambda i, e: (0, 0)),
                  pl.BlockSpec((1, d), lambda i, e: (0, 0))],
        out_specs=pl.BlockSpec((tm, d), lambda i, e: (i, 0)),
        scratch_shapes=[pltpu.VMEM((tm, d), F32), pltpu.VMEM((tm, LANES), F32)],
        compiler_params=_cparams("arbitrary", "arbitrary"),
    )(u2, logits, x1, wg, wu, wd, gt, ln_g.reshape(1, d), ln_b.reshape(1, d))


def _layer(x2d, mod, groups, rows_per_group, t_real, pos, w, attend, conv0, ssm0, tm, delta_tiles):
    m, d = x2d.shape
    heads = d // (2 * HEAD_DIM)
    alpha = float((2 * 1) ** 0.25)
    sh1, sc1, gt1, sh2, sc2, gt2 = jnp.split(mod, 6, axis=-1)
    if rows_per_group % tm == 0:
        expand = lambda a: a[:, None, :]
    else:
        expand = lambda a: jnp.repeat(a, rows_per_group, axis=0)
    sh1, sc1, gt1, sh2, sc2, gt2 = [expand(a) for a in (sh1, sc1, gt1, sh2, sc2, gt2)]
    c128, s128 = _rope_tables(pos, HEAD_DIM // 2)
    c64, s64 = _rope_tables(pos, IDX_DIM // 2)
    tabs = (c128, s128, c64, s64)
    if rows_per_group % tm != 0:
        tabs = tuple(jnp.tile(t, (tm // rows_per_group, 1)) for t in tabs)
    packed = _in_projection(x2d, sh1, sc1, w["w_in"], tabs, rows_per_group, tm)
    o_a = attend(packed)
    tci, tcc = delta_tiles
    o_b, conv_new, ssm_new = _delta_net(packed, conv0, ssm0, w["w_conv"], w["a_log"], w["dt_bias"],
                                        w["w_onorm"], groups, rows_per_group, t_real, tci, tcc, heads)
    x1, u2, logits = _out_projection(o_a, o_b, x2d, w["w_out_a"], w["w_out_b"], gt1, sh2, sc2,
                                     w["ln1_g"], w["ln1_b"], w["w_router"], w["b_router"],
                                     rows_per_group, min(tm, OUTPROJ_TM), alpha)
    x2 = _moe(u2, logits, x1, w["w_gate"], w["w_up"], w["w_down"], gt2, w["ln2_g"], w["ln2_b"],
              rows_per_group, min(tm, MOE_TM), alpha)
    return x2, packed, conv_new, ssm_new


def kernel(x_prompt, x_sample, cache_k, cache_v, cache_kidx, state_conv, state_ssm, page_table,
           c_prompt, c_sample, w_mod, b_mod, w_in, w_conv, a_log, dt_bias, w_onorm, w_out,
           ln1_g, ln1_b, w_grp, b_grp, w_erouter, b_erouter, w_gate, w_up, w_down, ln2_g, ln2_b):
    assert w_mod.shape[0] == 1, "single layer"
    batch, seq, d = x_prompt.shape
    nb, dec_seq, _ = x_sample.shape
    heads = d // (2 * HEAD_DIM)
    hw = heads * B_KEY_DIM
    past_len = page_table.shape[1] * cache_k.shape[2]
    kvw = A_KV_HEADS * HEAD_DIM

    n_rt = N_GROUPS + N_EXPERTS
    w_router = jnp.pad(jnp.concatenate([w_grp[0], w_erouter[0]], axis=1), ((0, 0), (0, LANES - n_rt)))
    b_router = jnp.pad(jnp.concatenate([b_grp[0], b_erouter[0]]), (0, LANES - n_rt)).reshape(1, LANES)
    w = dict(w_in=_pack_w_in(w_in[0], d), w_conv=w_conv[0], a_log=a_log[0], dt_bias=dt_bias[0],
             w_onorm=w_onorm[0], w_out_a=w_out[0, :hw].astype(BF16), w_out_b=w_out[0, hw:].astype(BF16),
             ln1_g=ln1_g[0], ln1_b=ln1_b[0], w_router=w_router, b_router=b_router,
             w_gate=w_gate[0].astype(BF16), w_up=w_up[0].astype(BF16), w_down=w_down[0].astype(BF16),
             ln2_g=ln2_g[0], ln2_b=ln2_b[0])

    n_c = batch + nb
    c_all = jnp.pad(jnp.concatenate([c_prompt, c_sample], axis=0), ((0, (-n_c) % SUBLANES), (0, 0)))
    mod = _modulation(c_all, w_mod[0], b_mod[0])

    def attend_prompt(packed):
        k_bf = packed[:, KA_OFF:KA_OFF + kvw].astype(BF16)
        v_bf = packed[:, VA_OFF:VA_OFF + kvw].astype(BF16)
        ki_bf = packed[:, SM_OFF + SM_KI:SM_OFF + SM_KI + IDX_DIM].astype(BF16)
        return _dsa_prompt(packed, k_bf, v_bf, ki_bf, batch, seq, heads)

    tm_p = min(INPROJ_TM, seq)
    conv0_p = jnp.zeros((batch, SUBLANES, 3 * hw), F32)
    ssm0_p = jnp.zeros((batch, heads, B_KEY_DIM, B_VAL_DIM), F32)
    tc_p = min(DELTA_TM, seq)
    y_p, packed_p, conv_p, ssm_p = _layer(
        x_prompt.reshape(batch * seq, d), mod[:batch], batch, seq, seq, jnp.arange(seq, dtype=I32), w,
        attend_prompt, conv0_p, ssm0_p, tm_p, (tc_p, tc_p))

    rows = SUBLANES
    assert CONV_WIDTH - 1 <= dec_seq <= rows
    x_s = jnp.pad(x_sample, ((0, 0), (0, rows - dec_seq), (0, 0))).reshape(nb * rows, d)
    pos_s = past_len + jnp.arange(rows, dtype=I32)

    def attend_sample(packed):
        o = _dsa_sample(packed.reshape(nb, rows, PACK_W), cache_k, cache_v, cache_kidx,
                        page_table, dec_seq, heads)
        return o.reshape(nb * rows, heads * HEAD_DIM)

    conv0_s = jnp.pad(state_conv[0], ((0, 0), (rows - (CONV_WIDTH - 1), 0), (0, 0)))
    tm_s = min(256, nb * rows)
    y_s, packed_s, conv_s, ssm_s = _layer(
        x_s, mod[batch:batch + nb], nb, rows, dec_seq, pos_s, w, attend_sample, conv0_s, state_ssm[0],
        tm_s, (rows, DELTA_CHUNK))

    def states(packed, groups, t_pad, t):
        p = packed.reshape(groups, t_pad, PACK_W)[:, :t]
        k = p[..., KA_OFF:KA_OFF + kvw].reshape(1, groups, t, A_KV_HEADS, HEAD_DIM)
        v = p[..., VA_OFF:VA_OFF + kvw].reshape(1, groups, t, A_KV_HEADS, HEAD_DIM)
        ki = p[..., SM_OFF + SM_KI:SM_OFF + SM_KI + IDX_DIM][None]
        return k, v, ki

    k_p, v_p, ki_p = states(packed_p, batch, seq, seq)
    k_s, v_s, ki_s = states(packed_s, nb, rows, dec_seq)
    conv_p = conv_p[None, :, rows - (CONV_WIDTH - 1):]
    conv_s = conv_s[None, :, rows - (CONV_WIDTH - 1):]
    y_s = y_s.reshape(nb, rows, d)[:, :dec_seq]
    return (y_p.reshape(batch, seq, d), y_s, k_p, v_p, ki_p, conv_p, ssm_p[None],
            k_s, v_s, ki_s, conv_s, ssm_s[None])
```

```python
import functools

import numpy as np
import jax
import jax.numpy as jnp
from jax import lax
from jax.experimental import pallas as pl
from jax.experimental.pallas import tpu as pltpu

F32 = jnp.float32
BF16 = jnp.bfloat16
I32 = jnp.int32

HEAD_DIM = 128
A_KV_HEADS = 2
IDX_HEADS = 16
IDX_DIM = 64
TOPK_MAX = 256
ROPE_THETA = 10000.0
B_KEY_DIM = 128
B_VAL_DIM = 128
CONV_WIDTH = 4
DELTA_CHUNK = 64
DELTA_GROUP = 4
N_GROUPS = 4
EXPERTS_PER_GROUP = 4
N_EXPERTS = N_GROUPS * EXPERTS_PER_GROUP
LN_EPS = 1e-5
RMS_EPS = 1e-6
L2_EPS = 1e-6

LANES = 128
SUBLANES = 8
VMEM_LIMIT = 56 * 1024 * 1024
INPROJ_TM = 1024
OUTPROJ_TM = 256
OUTPROJ_PARTS = 2
MOE_TM = 512
DELTA_TM = 256
DSA_TQ = 256
SAMPLE_PAGES_PER_STEP = 32

PACK_TN = 512
QA_OFF, QI_OFF, QB_OFF, KB_OFF, VB_OFF, Z_OFF, KA_OFF, VA_OFF, SM_OFF = (
    0, 1024, 2048, 3072, 4096, 5120, 6144, 6400, 6656)
PACK_W = 7168
SM_KI, SM_WI, SM_BETA, SM_DECAY = 0, 64, 80, 88
RT_GRP, RT_EXP = 0, N_GROUPS

NEG_BIG = -1e30
MASK_BIAS = -2e30
MAX_SEARCH_ITERS = 64
LOG2E = 1.4426950408889634
EXP2_SAFE_BOUND = 60.0
BOUND_SLACK = 1.02

NT_DIMS = (((1,), (1,)), ((), ()))
TN_DIMS = (((0,), (0,)), ((), ()))


def _cparams(*sem):
    return pltpu.CompilerParams(dimension_semantics=sem, vmem_limit_bytes=VMEM_LIMIT)


def _ln(x):
    mu = jnp.mean(x, axis=-1, keepdims=True)
    xc = x - mu
    return xc * lax.rsqrt(jnp.mean(xc * xc, axis=-1, keepdims=True) + LN_EPS)


def _sigmoid(x):
    return 1.0 / (1.0 + jnp.exp(-x))


def _silu(x):
    return x * _sigmoid(x)


def _dot(a, b):
    return jnp.dot(a.astype(BF16), b.astype(BF16), preferred_element_type=F32)


def _dot_nt(a, b):
    return lax.dot_general(a.astype(BF16), b.astype(BF16), NT_DIMS, preferred_element_type=F32)


def _split(a):
    hi = a.astype(BF16)
    lo = (a - hi.astype(F32)).astype(BF16)
    return hi, lo


def _dot3(a, b):
    ah, al = _split(a)
    bh, bl = _split(b)
    d = functools.partial(jnp.dot, preferred_element_type=F32)
    return d(ah, bh) + (d(ah, bl) + d(al, bh))


def _kth_threshold(count_ge, mn, mx, n_adm, topk):
    k = float(topk)
    all_in = n_adm <= k
    done0 = jnp.where(all_in | (count_ge(mx) >= k), 1.0, 0.0)
    thr0 = jnp.where(all_in, mn, mx)

    def cond(st):
        return (st[0] < MAX_SEARCH_ITERS) & (st[-1] > 0.0)

    def body(st):
        it, lo, hi, thr, done_f, _ = st
        done = done_f > 0.5
        mid = 0.5 * lo + 0.5 * hi
        stuck = jnp.logical_not((mid > lo) & (mid < hi))
        c = count_ge(mid)
        ge = c >= k
        hit = c == k
        thr = jnp.where(done, thr, jnp.where(hit, mid, jnp.where(stuck, lo, thr)))
        done_f = jnp.where(done | hit | stuck, 1.0, 0.0)
        return it + 1, jnp.where(ge, mid, lo), jnp.where(ge, hi, mid), thr, done_f, jnp.sum(1.0 - done_f)

    st = lax.while_loop(cond, body, (jnp.int32(0), mn, mx, thr0, done0, jnp.sum(1.0 - done0)))
    return jnp.where(st[4] > 0.5, st[3], st[1])


def _mod_kernel(c_ref, w_ref, b_ref, o_ref):
    o_ref[...] = _dot(_silu(c_ref[...]), w_ref[...]) + b_ref[...]


def _modulation(c, w_mod, b_mod):
    m, d = c.shape
    n = w_mod.shape[1]
    tn = 1024
    return pl.pallas_call(
        _mod_kernel,
        out_shape=jax.ShapeDtypeStruct((m, n), F32),
        grid=(n // tn,),
        in_specs=[pl.BlockSpec((m, d), lambda j: (0, 0)),
                  pl.BlockSpec((d, tn), lambda j: (0, j)),
                  pl.BlockSpec((1, tn), lambda j: (0, j))],
        out_specs=pl.BlockSpec((m, tn), lambda j: (0, j)),
        compiler_params=_cparams("arbitrary"),
    )(c, w_mod, b_mod.reshape(1, n))


def _rope_tables(pos, half):
    inv = ROPE_THETA ** (-jnp.arange(half, dtype=F32) / half)
    ang = pos.astype(F32)[:, None] * inv[None, :]
    c, s = jnp.cos(ang), jnp.sin(ang)
    rep = LANES // (2 * half)
    cos = jnp.tile(jnp.concatenate([c, c], -1), (1, rep))
    sin = jnp.tile(jnp.concatenate([-s, s], -1), (1, rep))
    return cos, sin


def _rope128(x, cos, sin):
    return x * cos + pltpu.roll(x, HEAD_DIM // 2, 1) * sin


def _rope64(x, cos, sin):
    lane = lax.broadcasted_iota(I32, x.shape, 1)
    first = (lane % IDX_DIM) < (IDX_DIM // 2)
    rot = jnp.where(first, pltpu.roll(x, LANES - IDX_DIM // 2, 1), pltpu.roll(x, IDX_DIM // 2, 1))
    return x * cos + rot * sin


def _inproj_kernel(x_ref, sh_ref, sc_ref, w_ref, c128_ref, s128_ref, c64_ref, s64_ref,
                   o_ref, u_ref, *, wi_scale):
    n = pl.program_id(1)

    @pl.when(n == 0)
    def _():
        y = _ln(x_ref[...]) * (1.0 + sc_ref[...]) + sh_ref[...]
        u_ref[...] = y.astype(BF16)

    acc = jnp.dot(u_ref[...], w_ref[...], preferred_element_type=F32)

    def kind(col):
        if col < QI_OFF or KA_OFF <= col < VA_OFF:
            return "rope128"
        if col < QB_OFF:
            return "rope64"
        return "small" if col == SM_OFF else "plain"

    def store(j, k):
        sl = slice(j * LANES, (j + 1) * LANES)
        a = acc[:, sl]
        if k == "rope128":
            a = _rope128(a, c128_ref[...], s128_ref[...])
        elif k == "rope64":
            a = _rope64(a, c64_ref[...], s64_ref[...])
        elif k == "small":
            lane = lax.broadcasted_iota(I32, a.shape, 1)
            r = _rope64(a, c64_ref[...], s64_ref[...])
            a = jnp.where(lane < SM_WI, r, jnp.where(lane < SM_BETA, a * wi_scale, a))
        o_ref[:, sl] = a

    sub = PACK_TN // LANES
    tile_kinds = [tuple(kind(t * PACK_TN + j * LANES) for j in range(sub)) for t in range(PACK_W // PACK_TN)]
    plain = ("plain",) * sub
    for kinds in sorted(set(tile_kinds) - {plain}):
        tiles = [t for t, k in enumerate(tile_kinds) if k == kinds]
        cond = functools.reduce(jnp.logical_or, [n == t for t in tiles])

        @pl.when(cond)
        def _(kinds=kinds):
            for j, k in enumerate(kinds):
                store(j, k)

    @pl.when(functools.reduce(jnp.logical_and, [n != t for t, k in enumerate(tile_kinds) if k != plain]))
    def _():
        o_ref[...] = acc


def _in_projection(x, sh, sc, w_pack, tabs, rows_per_group, tm):
    m, d = x.shape
    c128, s128, c64, s64 = tabs
    nt = PACK_W // PACK_TN
    if sh.ndim == 3:
        tpg = rows_per_group // tm
        mod_spec = pl.BlockSpec((None, 1, d), lambda i, n: (i // tpg, 0, 0))
    else:
        mod_spec = pl.BlockSpec((tm, d), lambda i, n: (i, 0))
    tab_blocks = c128.shape[0] // tm
    tab_spec = pl.BlockSpec((tm, LANES), lambda i, n: (i % tab_blocks, 0))
    wi_scale = float((IDX_HEADS * IDX_DIM) ** -0.5)
    return pl.pallas_call(
        functools.partial(_inproj_kernel, wi_scale=wi_scale),
        out_shape=jax.ShapeDtypeStruct((m, PACK_W), F32),
        grid=(m // tm, nt),
        in_specs=[pl.BlockSpec((tm, d), lambda i, n: (i, 0)), mod_spec, mod_spec,
                  pl.BlockSpec((None, d, PACK_TN), lambda i, n: (n, 0, 0)),
                  tab_spec, tab_spec, tab_spec, tab_spec],
        out_specs=pl.BlockSpec((tm, PACK_TN), lambda i, n: (i, n)),
        scratch_shapes=[pltpu.VMEM((tm, d), BF16)],
        compiler_params=_cparams("arbitrary", "arbitrary"),
    )(x, sh, sc, w_pack, c128, s128, c64, s64)


def _pack_w_in(w_in, d):
    heads = d // (2 * HEAD_DIM)
    conv_dim = 2 * heads * B_KEY_DIM + heads * B_VAL_DIM
    sizes = (heads * HEAD_DIM, A_KV_HEADS * HEAD_DIM, A_KV_HEADS * HEAD_DIM, IDX_HEADS * IDX_DIM,
             IDX_DIM, IDX_HEADS, conv_dim, heads * B_VAL_DIM, heads, heads)
    pts = [int(v) for v in np.cumsum(sizes)[:-1]]
    qa, ka, va, qi, ki, wi, qkv, z, braw, araw = jnp.split(w_in, pts, axis=1)
    small = jnp.concatenate([ki, wi, braw, araw], axis=1)
    small = jnp.pad(small, ((0, 0), (0, PACK_W - SM_OFF - small.shape[1])))
    w = jnp.concatenate([qa, qi, qkv, z, ka, va, small], axis=1)
    assert w.shape[1] == PACK_W
    return w.astype(BF16).reshape(d, PACK_W // PACK_TN, PACK_TN).transpose(1, 0, 2)


def _dsa_prompt_kernel(q_ref, qi_ref, sm_ref, k_ref, vt_ref, ki_ref, o_ref,
                       qt_ref, qit_ref, w_ref, sc_ref, s_ref, p_ref, m_ref, l_ref, a_ref, acc_ref,
                       b_ref, kmax_ref, lp_ref, *, tq, nq, topk, heads):
    i = pl.program_id(1)
    nkb = i + 1
    tk = tq
    scale = float(HEAD_DIM ** -0.5 * LOG2E)
    hpg = heads // A_KV_HEADS

    @pl.when(i == 0)
    def _():
        def body(kb, mxs):
            x = k_ref[pl.ds(pl.multiple_of(kb * tk, tk), tk), :].astype(F32)
            x2 = x * x
            return tuple(jnp.maximum(mxs[g], jnp.sum(x2[:, g * HEAD_DIM:(g + 1) * HEAD_DIM], axis=1, keepdims=True))
                         for g in range(A_KV_HEADS))
        mxs = lax.fori_loop(0, nq, body, (jnp.zeros((tk, 1), F32),) * A_KV_HEADS)
        for g in range(A_KV_HEADS):
            kmax_ref[g] = jnp.broadcast_to(jnp.sqrt(jnp.max(mxs[g], axis=0, keepdims=True)), kmax_ref.shape[1:])

    for h in range(heads):
        qt = jnp.transpose(q_ref[:, h * HEAD_DIM:(h + 1) * HEAD_DIM] * scale)
        qt_ref[h] = qt.astype(BF16)
        b_ref[h] = (jnp.sqrt(jnp.sum(qt * qt, axis=0, keepdims=True)) * kmax_ref[h // hpg, 0:1, :]) * BOUND_SLACK
    for hp in range(IDX_HEADS * IDX_DIM // LANES):
        t = jnp.transpose(qi_ref[:, hp * LANES:(hp + 1) * LANES])
        for u in range(LANES // IDX_DIM):
            qit_ref[hp * (LANES // IDX_DIM) + u] = t[u * IDX_DIM:(u + 1) * IDX_DIM].astype(BF16)
    w_ref[...] = jnp.transpose(sm_ref[...])[SM_WI:SM_WI + IDX_HEADS]
    qcol = i * tq + lax.broadcasted_iota(I32, (tk, tq), 1)
    krow = lax.broadcasted_iota(I32, (tk, tq), 0)

    def scores(kb, carry):
        mn, mx = carry
        off = pl.multiple_of(kb * tk, tk)
        kblk = ki_ref[pl.ds(off, tk), :]
        acc = jnp.zeros((tk, tq), F32)
        for h in range(IDX_HEADS):
            s = jnp.dot(kblk, qit_ref[h], preferred_element_type=F32)
            acc = acc + jnp.maximum(s, 0.0) * w_ref[h:h + 1, :]
        adm = krow + kb * tk <= qcol
        sc_ref[kb] = jnp.where(adm, acc, -jnp.inf)
        mn = jnp.minimum(mn, jnp.min(jnp.where(adm, acc, jnp.inf), axis=0, keepdims=True))
        mx = jnp.maximum(mx, jnp.max(jnp.where(adm, acc, -jnp.inf), axis=0, keepdims=True))
        return mn, mx

    mn, mx = lax.fori_loop(0, nkb, scores,
                           (jnp.full((1, tq), jnp.inf, F32), jnp.full((1, tq), -jnp.inf, F32)))

    sc_ref[nkb] = jnp.full((tk, tq), -jnp.inf, F32)

    def count_ge(t):
        def body(pi, cnt):
            for u in range(2):
                ind = jnp.where(sc_ref[2 * pi + u] >= t, 1.0, 0.0)
                cnt = cnt + jnp.sum(ind.reshape(tk // SUBLANES, SUBLANES, tq), axis=0)
            return cnt
        cnt = lax.fori_loop(0, (nkb + 1) // 2, body, jnp.zeros((SUBLANES, tq), F32))
        return jnp.sum(cnt, axis=0, keepdims=True)

    n_adm = (i * tq + lax.broadcasted_iota(I32, (1, tq), 1) + 1).astype(F32)
    thr = _kth_threshold(count_ge, mn, mx, n_adm, topk)
    acc_ref[...] = jnp.zeros(acc_ref.shape, F32)
    bmax = jnp.max(jnp.concatenate([b_ref[h] for h in range(heads)], axis=0))

    def keys_of(kb, h):
        g = h // hpg
        return k_ref[pl.ds(pl.multiple_of(kb * tk, tk), tk), g * HEAD_DIM:(g + 1) * HEAD_DIM]

    @pl.when(bmax <= EXP2_SAFE_BOUND)
    def _():
        lp_ref[...] = jnp.zeros(lp_ref.shape, F32)

        def attend(kb, c):
            bias = jnp.where(sc_ref[kb] >= thr, 0.0, MASK_BIAS)
            for h in range(heads):
                p = jnp.exp2(jnp.dot(keys_of(kb, h), qt_ref[h], preferred_element_type=F32) + (bias - b_ref[h]))
                lp_ref[h] += jnp.sum(p.reshape(tk // SUBLANES, SUBLANES, tq), axis=0)
                p_ref[h] = p.astype(BF16)
            for h in range(heads):
                acc_ref[h] += jnp.dot(vt_ref[kb, h // hpg], p_ref[h], preferred_element_type=F32)
            return c

        lax.fori_loop(0, nkb, attend, 0)
        for h in range(heads):
            l = jnp.sum(lp_ref[h], axis=0, keepdims=True)
            o_ref[:, h * HEAD_DIM:(h + 1) * HEAD_DIM] = jnp.transpose(acc_ref[h] / l).astype(BF16)

    @pl.when(bmax > EXP2_SAFE_BOUND)
    def _():
        m_ref[...] = jnp.full(m_ref.shape, NEG_BIG, F32)
        l_ref[...] = jnp.zeros(l_ref.shape, F32)

        def attend(kb, c):
            bias = jnp.where(sc_ref[kb] >= thr, 0.0, MASK_BIAS)
            for h in range(heads):
                s_ref[h] = jnp.dot(keys_of(kb, h), qt_ref[h], preferred_element_type=F32) + bias
            for h in range(heads):
                s = s_ref[h]
                m_old = m_ref[h]
                m_new = jnp.maximum(m_old, jnp.max(s, axis=0, keepdims=True))
                p = jnp.exp2(s - m_new)
                alpha = jnp.exp2(m_old - m_new)
                l_ref[h] = alpha * l_ref[h] + jnp.sum(p, axis=0, keepdims=True)
                p_ref[h] = p.astype(BF16)
                m_ref[h] = m_new
                a_ref[h] = alpha
            for h in range(heads):
                acc_ref[h] = a_ref[h] * acc_ref[h] + jnp.dot(vt_ref[kb, h // hpg], p_ref[h],
                                                             preferred_element_type=F32)
            return c

        lax.fori_loop(0, nkb, attend, 0)
        for h in range(heads):
            o_ref[:, h * HEAD_DIM:(h + 1) * HEAD_DIM] = jnp.transpose(acc_ref[h] / l_ref[h]).astype(BF16)


def _dsa_prompt(packed, k_bf, v_bf, ki_bf, batch, seq, heads):
    tq = min(DSA_TQ, seq)
    nq = seq // tq
    topk = min(TOPK_MAX, seq // 4)
    aw = heads * HEAD_DIM
    iw = IDX_HEADS * IDX_DIM
    kvw = A_KV_HEADS * HEAD_DIM
    vt = v_bf.reshape(batch, nq, tq, A_KV_HEADS, HEAD_DIM).transpose(0, 1, 3, 4, 2)
    return pl.pallas_call(
        functools.partial(_dsa_prompt_kernel, tq=tq, nq=nq, topk=topk, heads=heads),
        out_shape=jax.ShapeDtypeStruct((batch * seq, aw), BF16),
        grid=(batch, nq),
        in_specs=[pl.BlockSpec((tq, aw), lambda b, i: (b * nq + i, QA_OFF // aw)),
                  pl.BlockSpec((tq, iw), lambda b, i: (b * nq + i, QI_OFF // iw)),
                  pl.BlockSpec((tq, LANES), lambda b, i: (b * nq + i, SM_OFF // LANES)),
                  pl.BlockSpec((seq, kvw), lambda b, i: (b, 0)),
                  pl.BlockSpec((None, nq, A_KV_HEADS, HEAD_DIM, tq), lambda b, i: (b, 0, 0, 0, 0)),
                  pl.BlockSpec((seq, IDX_DIM), lambda b, i: (b, 0))],
        out_specs=pl.BlockSpec((tq, aw), lambda b, i: (b * nq + i, 0)),
        scratch_shapes=[pltpu.VMEM((heads, HEAD_DIM, tq), BF16),
                        pltpu.VMEM((IDX_HEADS, IDX_DIM, tq), BF16),
                        pltpu.VMEM((IDX_HEADS, tq), F32),
                        pltpu.VMEM((nq + 1, tq, tq), F32),
                        pltpu.VMEM((heads, tq, tq), F32),
                        pltpu.VMEM((heads, tq, tq), BF16),
                        pltpu.VMEM((heads, 1, tq), F32),
                        pltpu.VMEM((heads, 1, tq), F32),
                        pltpu.VMEM((heads, 1, tq), F32),
                        pltpu.VMEM((heads, HEAD_DIM, tq), F32),
                        pltpu.VMEM((heads, 1, tq), F32),
                        pltpu.VMEM((A_KV_HEADS, SUBLANES, tq), F32),
                        pltpu.VMEM((heads, SUBLANES, tq), F32)],
        compiler_params=_cparams("arbitrary", "arbitrary"),
    )(packed, packed, packed, k_bf, vt, ki_bf)


def _dsa_sample_kernel(pt_ref, q_ref, qi_ref, sm_ref, kn_ref, vn_ref, *rest,
                       pg, npg, past_len, t_real, topk, heads):
    del pt_ref
    kidx_refs = rest[:pg]
    k_refs = rest[pg:2 * pg]
    v_refs = rest[2 * pg:3 * pg]
    o_ref = rest[3 * pg]
    qi2_ref, w_ref, qt_ref, sc_ref, mm_ref, thr_ref, pad_ref, m_ref, l_ref, acc_ref = rest[3 * pg + 1:]
    j = pl.program_id(1)
    rows = SUBLANES
    ps = LANES
    scale = float(HEAD_DIM ** -0.5)
    hpg = heads // A_KV_HEADS

    def indexer(kt):
        n = kt.shape[1]
        s = jnp.dot(qi2_ref[...], kt.astype(BF16), preferred_element_type=F32)
        x = jnp.maximum(s, 0.0) * jnp.tile(w_ref[...], (1, n // LANES))
        return jnp.sum(x.reshape(IDX_HEADS, rows, n), axis=0)

    def lane_form(x):
        return jnp.transpose(jnp.tile(x, (LANES // rows, 1)))

    def fold_lanes(x, op):
        r = x[:, 0:LANES]
        for t in range(1, x.shape[1] // LANES):
            r = op(r, x[:, t * LANES:(t + 1) * LANES])
        return r

    def padded(block):
        pad_ref[...] = jnp.zeros(pad_ref.shape, F32)
        pad_ref[0:rows, 0:block.shape[1]] = block
        return pad_ref[:, 0:block.shape[1]]

    def softmax_step(s, v_of_group):
        m_old = m_ref[0:1, :]
        m_new = jnp.maximum(m_old, jnp.max(s, axis=0, keepdims=True))
        p = jnp.exp(s - m_new).astype(BF16)
        alpha = jnp.exp(m_old - m_new)
        l_ref[...] = jnp.broadcast_to(alpha * l_ref[0:1, :] + jnp.sum(p.astype(F32), axis=0, keepdims=True),
                                      l_ref.shape)
        m_ref[...] = jnp.broadcast_to(m_new, m_ref.shape)
        for g in range(A_KV_HEADS):
            acc_ref[g] = alpha * acc_ref[g] + lax.dot_general(
                v_of_group(g).astype(BF16), p, TN_DIMS, preferred_element_type=F32)

    @pl.when(j == 0)
    def _():
        for h in range(IDX_HEADS):
            qi2_ref[h * rows:(h + 1) * rows, :] = qi_ref[:, h * IDX_DIM:(h + 1) * IDX_DIM].astype(BF16)
            w_ref[h * rows:(h + 1) * rows, :] = jnp.broadcast_to(
                sm_ref[:, SM_WI + h:SM_WI + h + 1], (rows, LANES))
        for g in range(A_KV_HEADS):
            pad_ref[...] = jnp.zeros(pad_ref.shape, F32)
            for h in range(g * hpg, (g + 1) * hpg):
                pad_ref[h * rows:(h + 1) * rows, :] = q_ref[:, h * HEAD_DIM:(h + 1) * HEAD_DIM] * scale
            qt_ref[g] = jnp.transpose(pad_ref[...]).astype(BF16)
        mm_ref[0] = jnp.full((rows, LANES), jnp.inf, F32)
        mm_ref[1] = jnp.full((rows, LANES), -jnp.inf, F32)
        m_ref[...] = jnp.full(m_ref.shape, NEG_BIG, F32)
        l_ref[...] = jnp.zeros(l_ref.shape, F32)
        acc_ref[...] = jnp.zeros(acc_ref.shape, F32)

    @pl.when(j < npg)
    def _():
        x = indexer(jnp.concatenate([r[...] for r in kidx_refs], axis=1))
        sc_ref[j] = x
        mm_ref[0] = jnp.minimum(mm_ref[0], fold_lanes(x, jnp.minimum))
        mm_ref[1] = jnp.maximum(mm_ref[1], fold_lanes(x, jnp.maximum))

    @pl.when(j == npg - 1)
    def _():
        kt_new = jnp.transpose(padded(sm_ref[...]))[SM_KI:SM_KI + IDX_DIM]
        x = indexer(kt_new)
        qrow = lax.broadcasted_iota(I32, (rows, ps), 0)
        kcol = lax.broadcasted_iota(I32, (rows, ps), 1)
        adm = (kcol <= qrow) & (kcol < t_real)
        sc_ref[npg] = jnp.full(sc_ref.shape[1:], -jnp.inf, F32)
        sc_ref[npg, :, 0:ps] = jnp.where(adm, x, -jnp.inf)
        mn = jnp.min(jnp.minimum(mm_ref[0], jnp.where(adm, x, jnp.inf)), axis=1, keepdims=True)
        mx = jnp.max(jnp.maximum(mm_ref[1], jnp.where(adm, x, -jnp.inf)), axis=1, keepdims=True)

        def count_ge(t):
            def body(b, cnt):
                return cnt + fold_lanes(jnp.where(sc_ref[b] >= t, 1.0, 0.0), jnp.add)
            cnt = lax.fori_loop(0, npg + 1, body, jnp.zeros((rows, LANES), F32))
            return jnp.sum(cnt, axis=1, keepdims=True)

        q1 = lax.broadcasted_iota(I32, (rows, 1), 0)
        n_adm = (past_len + jnp.minimum(q1, t_real - 1) + 1).astype(F32)
        thr = _kth_threshold(count_ge, mn, mx, n_adm, topk)
        thr_ref[...] = lane_form(jnp.broadcast_to(thr, (rows, LANES)))[0:rows]

    @pl.when(j >= npg)
    def _():
        bias = jnp.where(lane_form(sc_ref[j - npg]) >= thr_ref[0:1, :], 0.0, MASK_BIAS)
        s = bias
        for g in range(A_KV_HEADS):
            kg = jnp.concatenate([r[pl.ds(g, ps, stride=A_KV_HEADS), :] for r in k_refs], axis=0)
            s = s + jnp.dot(kg.astype(BF16), qt_ref[g], preferred_element_type=F32)
        softmax_step(s, lambda g: jnp.concatenate([r[pl.ds(g, ps, stride=A_KV_HEADS), :] for r in v_refs], axis=0))

    @pl.when(j == 2 * npg - 1)
    def _():
        s = jnp.where(lane_form(sc_ref[npg, :, 0:ps]) >= thr_ref[0:1, :], 0.0, MASK_BIAS)
        for g in range(A_KV_HEADS):
            kg = padded(kn_ref[:, g * HEAD_DIM:(g + 1) * HEAD_DIM])
            s = s + jnp.dot(kg.astype(BF16), qt_ref[g], preferred_element_type=F32)
        softmax_step(s, lambda g: padded(vn_ref[:, g * HEAD_DIM:(g + 1) * HEAD_DIM]))
        for g in range(A_KV_HEADS):
            out = jnp.transpose(acc_ref[g] / l_ref[0:1, :])
            for h in range(g * hpg, (g + 1) * hpg):
                o_ref[:, h * HEAD_DIM:(h + 1) * HEAD_DIM] = out[h * rows:(h + 1) * rows, :].astype(BF16)


def _dsa_sample(packed3, cache_k, cache_v, cache_kidx, page_table, t_real, heads):
    nb, rows, _ = packed3.shape
    assert rows == SUBLANES and IDX_HEADS * rows == LANES
    ps = cache_k.shape[2]
    assert ps == LANES and cache_k.shape[0] == 1
    n_pages = page_table.shape[1]
    pg = min(SAMPLE_PAGES_PER_STEP, n_pages)
    npg = n_pages // pg
    assert n_pages % pg == 0
    kvw = A_KV_HEADS * HEAD_DIM
    aw = heads * HEAD_DIM
    iw = IDX_HEADS * IDX_DIM
    past_len = n_pages * ps
    topk = min(TOPK_MAX, (past_len + t_real) // 4)
    pt = page_table.reshape(-1).astype(I32)

    def page1(b, j, pt, i):
        return pt[b * n_pages + jnp.minimum(j, npg - 1) * pg + i]

    def page2(b, j, pt, i):
        return pt[b * n_pages + jnp.maximum(j - npg, 0) * pg + i]

    kidx_t = jnp.swapaxes(cache_kidx, 2, 3)

    def kidx_spec(i):
        return pl.BlockSpec((None, None, IDX_DIM, ps), lambda b, j, pt: (0, page1(b, j, pt, i), 0, 0))

    ck = cache_k.reshape(cache_k.shape[1], ps * A_KV_HEADS, HEAD_DIM)
    cv = cache_v.reshape(cache_v.shape[1], ps * A_KV_HEADS, HEAD_DIM)

    def kv_spec(i):
        return pl.BlockSpec((None, ps * A_KV_HEADS, HEAD_DIM),
                            lambda b, j, pt: (page2(b, j, pt, i), 0, 0))

    in_specs = [pl.BlockSpec((None, rows, aw), lambda b, j, pt: (b, 0, QA_OFF // aw)),
                pl.BlockSpec((None, rows, iw), lambda b, j, pt: (b, 0, QI_OFF // iw)),
                pl.BlockSpec((None, rows, LANES), lambda b, j, pt: (b, 0, SM_OFF // LANES)),
                pl.BlockSpec((None, rows, kvw), lambda b, j, pt: (b, 0, KA_OFF // kvw)),
                pl.BlockSpec((None, rows, kvw), lambda b, j, pt: (b, 0, VA_OFF // kvw))]
    in_specs += [kidx_spec(i) for i in range(pg)]
    in_specs += [kv_spec(i) for i in range(pg)]
    in_specs += [kv_spec(i) for i in range(pg)]
    return pl.pallas_call(
        functools.partial(_dsa_sample_kernel, pg=pg, npg=npg, past_len=past_len, t_real=t_real,
                          topk=topk, heads=heads),
        out_shape=jax.ShapeDtypeStruct((nb, rows, aw), BF16),
        grid_spec=pltpu.PrefetchScalarGridSpec(
            num_scalar_prefetch=1,
            grid=(nb, 2 * npg),
            in_specs=in_specs,
            out_specs=pl.BlockSpec((None, rows, aw), lambda b, j, pt: (b, 0, 0)),
            scratch_shapes=[pltpu.VMEM((IDX_HEADS * rows, IDX_DIM), BF16),
                            pltpu.VMEM((IDX_HEADS * rows, LANES), F32),
                            pltpu.VMEM((A_KV_HEADS, HEAD_DIM, LANES), BF16),
                            pltpu.VMEM((npg + 1, rows, pg * ps), F32),
                            pltpu.VMEM((2, rows, LANES), F32),
                            pltpu.VMEM((rows, LANES), F32),
                            pltpu.VMEM((LANES, LANES), F32),
                            pltpu.VMEM((rows, LANES), F32),
                            pltpu.VMEM((rows, LANES), F32),
                            pltpu.VMEM((A_KV_HEADS, HEAD_DIM, LANES), F32)]),
        compiler_params=_cparams("arbitrary", "arbitrary"),
    )(pt, packed3, packed3, packed3, packed3, packed3,
      *([kidx_t] * pg), *([ck] * pg), *([cv] * pg))


def _split3(x):
    x1 = x.astype(BF16)
    r = x - x1.astype(F32)
    x2 = r.astype(BF16)
    return x1, x2, (r - x2.astype(F32)).astype(BF16)


def _exact_right(x, m01):
    d = functools.partial(jnp.dot, preferred_element_type=F32)
    x1, x2, x3 = _split3(x)
    return (d(x1, m01) + d(x2, m01)) + d(x3, m01)


def _exact_left(m01, x):
    d = functools.partial(jnp.dot, preferred_element_type=F32)
    x1, x2, x3 = _split3(x)
    return (d(m01, x1) + d(m01, x2)) + d(m01, x3)


def _dot3s(ah, al, bh, bl):
    d = functools.partial(jnp.dot, preferred_element_type=F32)
    return d(ah, bh) + (d(ah, bl) + d(al, bh))


def _blockdiag_mask(nb, rb, cb):
    shape = (nb * rb, nb * cb)
    return lax.broadcasted_iota(I32, shape, 0) // rb == lax.broadcasted_iota(I32, shape, 1) // cb


def _blockdiag(x, mask):
    t = jnp.tile(x, (mask.shape[0] // x.shape[0], 1))
    return jnp.where(mask, t, jnp.zeros_like(t))


def _unit_lower_inverse_cat(a_cats, c, mask):
    shape = a_cats[0].shape
    eye = (lax.broadcasted_iota(I32, shape, 0) == lax.broadcasted_iota(I32, shape, 1) % c).astype(F32)

    def bd_halves(p):
        ph, pl_ = _split(p)
        return ph, pl_, _blockdiag(ph, mask), _blockdiag(pl_, mask)

    xs = [eye - a for a in a_cats]
    ps = []
    for a in a_cats:
        ph, pl_, bh, bl = bd_halves(a)
        ps.append(_dot3s(ph, pl_, bh, bl))
    n = 2
    while n < c:
        nxt_x, nxt_p = [], []
        for x, p in zip(xs, ps):
            _, _, bh, bl = bd_halves(p)
            if 2 * n < c:
                lh, ll = _split(jnp.concatenate([x, p], axis=0))
                r = _dot3s(lh, ll, bh, bl)
                nxt_x.append(x + r[0:c])
                nxt_p.append(r[c:2 * c])
            else:
                xh, xl = _split(x)
                nxt_x.append(x + _dot3s(xh, xl, bh, bl))
                nxt_p.append(p)
        xs, ps = nxt_x, nxt_p
        n *= 2
    d = functools.partial(jnp.dot, preferred_element_type=F32)
    res = []
    for a, x in zip(a_cats, xs):
        t1, t2, t3 = _split3(eye + a)
        x1, x2, x3 = (_blockdiag(v, mask) for v in _split3(x))
        tx = ((d(t1, x1) + d(t1, x2)) + (d(t2, x1) + d(t1, x3))) + (d(t2, x2) + d(t3, x1))
        res.append(eye - tx)
    out = []
    for x, r in zip(xs, res):
        xh, xl = _split(x)
        rh, rl = _split(r)
        out.append(x + _dot3s(xh, xl, _blockdiag(rh, mask), _blockdiag(rl, mask)))
    return out


def _delta_kernel(qb_ref, kb_ref, vb_ref, z_ref, sm_ref, conv0_ref, wconv_ref, alog_ref, dtb_ref,
                  wnorm_ref, ssm0_ref, eb_ref, eg_ref, egc_ref, o_ref, convo_ref, ssmo_ref, xbuf_ref, s_ref,
                  *, tci, tcc, t_real, heads):
    j = pl.program_id(1)
    c = DELTA_CHUNK
    hw = heads * B_KEY_DIM
    halo = SUBLANES
    hg = DELTA_GROUP
    gw, cw, pw = hg * B_KEY_DIM, hg * c, 2 * B_KEY_DIM

    @pl.when(j == 0)
    def _():
        xbuf_ref[0:halo, :] = conv0_ref[...]
        s_ref[...] = jnp.zeros(s_ref.shape, F32)
        for h in range(heads):
            o = (h % 2) * B_KEY_DIM
            s_ref[h // 2, o:o + B_KEY_DIM, o:o + B_VAL_DIM] = ssm0_ref[h]

    @pl.when(j > 0)
    def _():
        xbuf_ref[0:halo, :] = xbuf_ref[tcc:tcc + halo, :]

    xbuf_ref[halo:halo + tci, 0:hw] = qb_ref[...]
    xbuf_ref[halo:halo + tci, hw:2 * hw] = kb_ref[...]
    xbuf_ref[halo:halo + tci, 2 * hw:3 * hw] = vb_ref[...]
    if tci < tcc:
        xbuf_ref[halo + tci:halo + tcc, :] = jnp.zeros((tcc - tci, 3 * hw), F32)

    jr, r1 = (t_real - 1) // tcc, t_real - ((t_real - 1) // tcc) * tcc

    @pl.when(j == jr)
    def _():
        convo_ref[...] = jnp.zeros(convo_ref.shape, F32)
        convo_ref[halo - (CONV_WIDTH - 1):halo, :] = xbuf_ref[halo + r1 - (CONV_WIDTH - 1):halo + r1, :]

    conv = xbuf_ref[halo - 3:halo - 3 + tcc, :] * wconv_ref[0:1, :]
    for t in range(1, CONV_WIDTH):
        conv = conv + xbuf_ref[halo - 3 + t:halo - 3 + t + tcc, :] * wconv_ref[t:t + 1, :]
    act = _silu(conv)

    sm = sm_ref[...]
    if tci < tcc:
        sm = jnp.concatenate([sm, jnp.zeros((tcc - tci, LANES), F32)], axis=0)
    lane = lax.broadcasted_iota(I32, (tcc, LANES), 1)
    trow = j * tcc + lax.broadcasted_iota(I32, (tcc, LANES), 0)
    valid = trow < t_real
    beta_t = jnp.where(valid & (lane >= SM_BETA) & (lane < SM_BETA + heads), _sigmoid(sm), 0.0)
    xg = sm + dtb_ref[...]
    softplus = jnp.maximum(xg, 0.0) + jnp.log1p(jnp.exp(-jnp.abs(xg)))
    lg_t = jnp.where(valid & (lane >= SM_DECAY) & (lane < SM_DECAY + heads),
                     -jnp.exp(alog_ref[...]) * softplus, 0.0)
    ri = lax.broadcasted_iota(I32, (tcc, tcc), 0)
    ci = lax.broadcasted_iota(I32, (tcc, tcc), 1)
    tri = ((ci <= ri) & (ci // c == ri // c)).astype(BF16)
    g_t = _exact_left(tri, lg_t)
    rmask = valid[:, 0:1]

    def l2norm_heads(x, scale):
        parts = []
        for h in range(heads):
            xh = x[:, h * B_KEY_DIM:(h + 1) * B_KEY_DIM]
            parts.append(xh * (lax.rsqrt(jnp.sum(xh * xh, axis=-1, keepdims=True) + L2_EPS) * scale))
        return jnp.concatenate(parts, axis=1)

    qn_all = l2norm_heads(act[:, 0:hw], float(B_KEY_DIM ** -0.5))
    kn_all = l2norm_heads(act[:, hw:2 * hw], 1.0)
    v_all = act[:, 2 * hw:3 * hw]
    if t_real % tcc != 0 or tci < tcc:
        kn_all = jnp.where(rmask, kn_all, 0.0)
        v_all = jnp.where(rmask, v_all, 0.0)
    beta_e = _exact_right(beta_t, eb_ref[...])
    g_e = _exact_right(g_t, eg_ref[...])
    gc_all = _exact_right(g_t, egc_ref[...])
    eg_e = jnp.exp(g_e)
    kbeta_all = kn_all * beta_e
    vbeta_all = v_all * beta_e
    kbe_all = kbeta_all * eg_e
    qe_all = qn_all * eg_e

    crow = lax.broadcasted_iota(I32, (c, heads * c), 0)
    ccol = lax.broadcasted_iota(I32, (c, heads * c), 1) % c
    strict = (lax.broadcasted_iota(I32, (c, cw), 0) > lax.broadcasted_iota(I32, (c, cw), 1) % c)
    ones_cc = jnp.ones((c, c), BF16)
    assert B_KEY_DIM == B_VAL_DIM
    mask_c = _blockdiag_mask(hg, c, c)
    mask_d = _blockdiag_mask(hg, c, B_KEY_DIM)
    prow = lax.broadcasted_iota(I32, (pw, pw), 0) // B_KEY_DIM
    pcol = lax.broadcasted_iota(I32, (pw, pw), 1) // B_VAL_DIM
    units = [(ch, grp) for ch in range(tcc // c) for grp in range(heads // hg)]
    a_cats, qk_cats = [], []
    for ch in range(tcc // c):
        r = slice(ch * c, (ch + 1) * c)
        gc = gc_all[r]
        grow = _exact_left(ones_cc, jnp.where(crow == ccol, gc, 0.0))
        decay = jnp.where(crow >= ccol, jnp.exp(jnp.minimum(gc - grow, 0.0)), 0.0)
        for grp in range(heads // hg):
            ls = slice(grp * gw, (grp + 1) * gw)
            dec = decay[:, grp * cw:(grp + 1) * cw]
            kn_bd = _blockdiag(kn_all[r, ls].astype(BF16), mask_d)
            lhs = jnp.concatenate([kbeta_all[r, ls], qn_all[r, ls]], axis=0).astype(BF16)
            aq = lax.dot_general(lhs, kn_bd, NT_DIMS, preferred_element_type=F32)
            a_cats.append(jnp.where(strict, aq[0:c] * dec, 0.0))
            qk_cats.append(aq[c:2 * c] * dec)
    tinvs = _unit_lower_inverse_cat(a_cats, c, mask_c)
    us, ws = [], []
    for (ch, grp), tinv in zip(units, tinvs):
        r = slice(ch * c, (ch + 1) * c)
        ls = slice(grp * gw, (grp + 1) * gw)
        th, tl = _split(tinv)
        vh_, vl_ = _split(vbeta_all[r, ls])
        kh_, kl_ = _split(kbe_all[r, ls])
        us.append(_dot3s(th, tl, _blockdiag(vh_, mask_d), _blockdiag(vl_, mask_d)))
        ws.append(_dot3s(th, tl, _blockdiag(kh_, mask_d), _blockdiag(kl_, mask_d)))

    for ch in range(tcc // c):
        r = slice(ch * c, (ch + 1) * c)
        glast = g_e[(ch + 1) * c - 1:(ch + 1) * c, :]
        kdec_all = kn_all[r] * jnp.exp(glast - g_e[r])
        eglast = jnp.exp(glast)
        for grp in range(heads // hg):
            un = units.index((ch, grp))
            u, w, qk_cat = us[un], ws[un], qk_cats[un]
            vnew, qs = [], []
            for pr in range(hg // 2):
                idx = grp * (hg // 2) + pr
                la = slice(grp * gw + pr * pw, grp * gw + (pr + 1) * pw)
                lg_ = slice(pr * pw, (pr + 1) * pw)
                s_old = s_ref[idx]
                lhs2 = jnp.concatenate([w[:, lg_], qe_all[r, la]], axis=0).astype(BF16)
                rs = jnp.dot(lhs2, s_old.astype(BF16), preferred_element_type=F32)
                vn = u[:, lg_] - rs[0:c]
                upd = lax.dot_general(kdec_all[:, la].astype(BF16), vn.astype(BF16), TN_DIMS,
                                      preferred_element_type=F32)
                s_ref[idx] = s_old * eglast[:, la] + jnp.where(prow == pcol, upd, 0.0)
                vnew.append(vn)
                qs.append(rs[c:2 * c])
            vn_bd = _blockdiag(jnp.concatenate(vnew, axis=1).astype(BF16), mask_d)
            o_g = jnp.concatenate(qs, axis=1) + jnp.dot(qk_cat.astype(BF16), vn_bd, preferred_element_type=F32)
            if ch * c < tci:
                ro = slice(ch * c, min((ch + 1) * c, tci))
                nr = ro.stop - ro.start
                for a in range(hg):
                    h = grp * hg + a
                    oh = o_g[0:nr, a * B_VAL_DIM:(a + 1) * B_VAL_DIM]
                    zz = z_ref[ro, h * B_VAL_DIM:(h + 1) * B_VAL_DIM]
                    on = oh * lax.rsqrt(jnp.mean(oh * oh, axis=-1, keepdims=True) + RMS_EPS)
                    o_ref[ro, h * B_VAL_DIM:(h + 1) * B_VAL_DIM] = (on * wnorm_ref[...] * _silu(zz)).astype(BF16)

    @pl.when(j == pl.num_programs(1) - 1)
    def _():
        for h in range(heads):
            o = (h % 2) * B_KEY_DIM
            ssmo_ref[h] = s_ref[h // 2, o:o + B_KEY_DIM, o:o + B_VAL_DIM]


def _delta_net(packed, conv0, ssm0, w_conv, a_log, dt_bias, w_onorm, batch, t_pad, t_real, tci, tcc, heads):
    hw = heads * B_KEY_DIM
    nt = t_pad // tci
    assert t_pad % tci == 0 and tcc % DELTA_CHUNK == 0 and tci % SUBLANES == 0
    assert (nt == 1 and tci <= tcc) or tci == tcc
    assert heads % DELTA_GROUP == 0 and DELTA_GROUP % 2 == 0
    wconv_p = jnp.pad(w_conv, ((0, SUBLANES - CONV_WIDTH), (0, 0)))
    alog_row = jnp.zeros((1, LANES), F32).at[0, SM_DECAY:SM_DECAY + heads].set(a_log)
    dtb_row = jnp.zeros((1, LANES), F32).at[0, SM_DECAY:SM_DECAY + heads].set(dt_bias)
    def spread(src, width):
        m = np.zeros((LANES, heads * width), np.float32)
        for h in range(heads):
            m[src + h, h * width:(h + 1) * width] = 1.0
        return jnp.asarray(m, BF16)
    e_beta, e_g, e_gc = spread(SM_BETA, B_KEY_DIM), spread(SM_DECAY, B_KEY_DIM), spread(SM_DECAY, DELTA_CHUNK)
    full2 = lambda a: pl.BlockSpec(a.shape, lambda b, j: (0, 0))
    row = lambda b, j: (b * nt + j)
    return pl.pallas_call(
        functools.partial(_delta_kernel, tci=tci, tcc=tcc, t_real=t_real, heads=heads),
        out_shape=(jax.ShapeDtypeStruct((batch * t_pad, hw), BF16),
                   jax.ShapeDtypeStruct((batch, SUBLANES, 3 * hw), F32),
                   jax.ShapeDtypeStruct((batch, heads, B_KEY_DIM, B_VAL_DIM), F32)),
        grid=(batch, nt),
        in_specs=[pl.BlockSpec((tci, hw), lambda b, j: (row(b, j), QB_OFF // hw)),
                  pl.BlockSpec((tci, hw), lambda b, j: (row(b, j), KB_OFF // hw)),
                  pl.BlockSpec((tci, hw), lambda b, j: (row(b, j), VB_OFF // hw)),
                  pl.BlockSpec((tci, hw), lambda b, j: (row(b, j), Z_OFF // hw)),
                  pl.BlockSpec((tci, LANES), lambda b, j: (row(b, j), SM_OFF // LANES)),
                  pl.BlockSpec((None, SUBLANES, 3 * hw), lambda b, j: (b, 0, 0)),
                  pl.BlockSpec((SUBLANES, 3 * hw), lambda b, j: (0, 0)),
                  pl.BlockSpec((1, LANES), lambda b, j: (0, 0)),
                  pl.BlockSpec((1, LANES), lambda b, j: (0, 0)),
                  pl.BlockSpec((1, B_VAL_DIM), lambda b, j: (0, 0)),
                  pl.BlockSpec((None, heads, B_KEY_DIM, B_VAL_DIM), lambda b, j: (b, 0, 0, 0)),
                  full2(e_beta), full2(e_g), full2(e_gc)],
        out_specs=(pl.BlockSpec((tci, hw), lambda b, j: (row(b, j), 0)),
                   pl.BlockSpec((None, SUBLANES, 3 * hw), lambda b, j: (b, 0, 0)),
                   pl.BlockSpec((None, heads, B_KEY_DIM, B_VAL_DIM), lambda b, j: (b, 0, 0, 0))),
        scratch_shapes=[pltpu.VMEM((tcc + SUBLANES, 3 * hw), F32),
                        pltpu.VMEM((heads // 2, 2 * B_KEY_DIM, 2 * B_VAL_DIM), F32)],
        compiler_params=_cparams("arbitrary", "arbitrary"),
    )(packed, packed, packed, packed, packed, conv0, wconv_p, alog_row, dtb_row,
      w_onorm.reshape(1, B_VAL_DIM), ssm0, e_beta, e_g, e_gc)


def _outproj_kernel(oa_ref, ob_ref, x_ref, wa_ref, wb_ref, gt_ref, sh_ref, sc_ref, g_ref, b_ref,
                    wr_ref, br_ref, x1_ref, u2_ref, lg_ref, *, alpha):
    tm = x_ref.shape[0]
    parts = [slice(p * tm // OUTPROJ_PARTS, (p + 1) * tm // OUTPROJ_PARTS) for p in range(OUTPROJ_PARTS)]

    def rows(ref, r):
        return ref[r, :] if ref.shape[0] == tm else ref[...]

    mixes = [jnp.dot(oa_ref[r, :], wa_ref[...], preferred_element_type=F32)
             + jnp.dot(ob_ref[r, :], wb_ref[...], preferred_element_type=F32) for r in parts]
    for r, mix in zip(parts, mixes):
        x1 = _ln(alpha * x_ref[r, :] + rows(gt_ref, r) * mix) * g_ref[...] + b_ref[...]
        x1_ref[r, :] = x1
        u2 = _ln(x1) * (1.0 + rows(sc_ref, r)) + rows(sh_ref, r)
        u2_ref[r, :] = u2.astype(BF16)
        lg_ref[r, :] = _dot3(u2, wr_ref[...]) + br_ref[...]


def _out_projection(oa, ob, x, wa, wb, gt, sh, sc, ln_g, ln_b, w_r, b_r, rows_per_group, tm, alpha):
    m, d = x.shape
    hwid = oa.shape[1]
    if gt.ndim == 3:
        tpg = rows_per_group // tm
        mod_spec = pl.BlockSpec((None, 1, d), lambda i: (i // tpg, 0, 0))
    else:
        mod_spec = pl.BlockSpec((tm, d), lambda i: (i, 0))
    full = lambda r, c: pl.BlockSpec((r, c), lambda i: (0, 0))
    return pl.pallas_call(
        functools.partial(_outproj_kernel, alpha=alpha),
        out_shape=(jax.ShapeDtypeStruct((m, d), F32), jax.ShapeDtypeStruct((m, d), BF16),
                   jax.ShapeDtypeStruct((m, LANES), F32)),
        grid=(m // tm,),
        in_specs=[pl.BlockSpec((tm, hwid), lambda i: (i, 0)), pl.BlockSpec((tm, hwid), lambda i: (i, 0)),
                  pl.BlockSpec((tm, d), lambda i: (i, 0)), full(hwid, d), full(hwid, d),
                  mod_spec, mod_spec, mod_spec, full(1, d), full(1, d), full(d, LANES), full(1, LANES)],
        out_specs=(pl.BlockSpec((tm, d), lambda i: (i, 0)), pl.BlockSpec((tm, d), lambda i: (i, 0)),
                   pl.BlockSpec((tm, LANES), lambda i: (i, 0))),
        compiler_params=_cparams("arbitrary"),
    )(oa, ob, x, wa, wb, gt, sh, sc, ln_g.reshape(1, d), ln_b.reshape(1, d), w_r, b_r)


def _combine_weights(lg):
    lane = lax.broadcasted_iota(I32, lg.shape, 1)
    big = jnp.int32(LANES)
    gmask = lane < N_GROUPS
    gl = jnp.where(gmask, lg, -jnp.inf)
    gmax = jnp.max(gl, axis=-1, keepdims=True)
    g_idx = jnp.min(jnp.where(gl == gmax, lane, big), axis=-1, keepdims=True)
    p_top = 1.0 / jnp.sum(jnp.where(gmask, jnp.exp(gl - gmax), 0.0), axis=-1, keepdims=True)
    e_lane = lane - RT_EXP
    in_grp = (e_lane >= 0) & (e_lane < N_EXPERTS) & ((e_lane // EXPERTS_PER_GROUP) == g_idx)
    v = jnp.where(in_grp, lg, -jnp.inf)
    v1 = jnp.max(v, axis=-1, keepdims=True)
    i1 = jnp.min(jnp.where(v == v1, lane, big), axis=-1, keepdims=True)
    vr = jnp.where(lane == i1, -jnp.inf, v)
    v2 = jnp.max(vr, axis=-1, keepdims=True)
    i2 = jnp.min(jnp.where(vr == v2, lane, big), axis=-1, keepdims=True)
    e2 = jnp.exp(v2 - v1)
    den = 1.0 / (1.0 + e2)
    return jnp.where(lane == i1, den * p_top, jnp.where(lane == i2, e2 * den * p_top, 0.0))


def _moe_kernel(u_ref, lg_ref, x1_ref, wg_ref, wu_ref, wd_ref, gt_ref, g_ref, b_ref, o_ref,
                acc_ref, cmb_ref, *, alpha):
    e = pl.program_id(1)

    @pl.when(e == 0)
    def _():
        cmb_ref[...] = _combine_weights(lg_ref[...])
        acc_ref[...] = jnp.zeros(acc_ref.shape, F32)

    u = u_ref[...]
    h = jnp.dot(u, wg_ref[...], preferred_element_type=F32)
    up = jnp.dot(u, wu_ref[...], preferred_element_type=F32)
    lane = lax.broadcasted_iota(I32, cmb_ref.shape, 1)
    c_e = jnp.sum(jnp.where(lane == e + RT_EXP, cmb_ref[...], 0.0), axis=-1, keepdims=True)
    act = _silu(h) * up * c_e
    acc_ref[...] += jnp.dot(act.astype(BF16), wd_ref[...], preferred_element_type=F32)

    @pl.when(e == pl.num_programs(1) - 1)
    def _():
        o_ref[...] = _ln(alpha * x1_ref[...] + gt_ref[...] * acc_ref[...]) * g_ref[...] + b_ref[...]


def _moe(u2, logits, x1, wg, wu, wd, gt, ln_g, ln_b, rows_per_group, tm, alpha):
    m, d = x1.shape
    ne, _, de = wg.shape
    if gt.ndim == 3:
        tpg = rows_per_group // tm
        mod_spec = pl.BlockSpec((None, 1, d), lambda i, e: (i // tpg, 0, 0))
    else:
        mod_spec = pl.BlockSpec((tm, d), lambda i, e: (i, 0))
    return pl.pallas_call(
        functools.partial(_moe_kernel, alpha=alpha),
        out_shape=jax.ShapeDtypeStruct((m, d), F32),
        grid=(m // tm, ne),
        in_specs=[pl.BlockSpec((tm, d), lambda i, e: (i, 0)),
                  pl.BlockSpec((tm, LANES), lambda i, e: (i, 0)),
                  pl.BlockSpec((tm, d), lambda i, e: (i, 0)),
                  pl.BlockSpec((None, d, de), lambda i, e: (e, 0, 0)),
                  pl.BlockSpec((None, d, de), lambda i, e: (e, 0, 0)),
                  pl.BlockSpec((None, de, d), lambda i, e: (e, 0, 0)),
                  mod_spec,
                  pl.BlockSpec((1, d), lambda i, e: (0, 0)),
                  pl.BlockSpec((1, d), lambda i, e: (0, 0))],
        out_specs=pl.BlockSpec((tm, d), lambda i, e: (i, 0)),
        scratch_shapes=[pltpu.VMEM((tm, d), F32), pltpu.VMEM((tm, LANES), F32)],
        compiler_params=_cparams("arbitrary", "arbitrary"),
    )(u2, logits, x1, wg, wu, wd, gt, ln_g.reshape(1, d), ln_b.reshape(1, d))


def _layer(x2d, mod, groups, rows_per_group, t_real, pos, w, attend, conv0, ssm0, tm, delta_tiles):
    m, d = x2d.shape
    heads = d // (2 * HEAD_DIM)
    alpha = float((2 * 1) ** 0.25)
    sh1, sc1, gt1, sh2, sc2, gt2 = jnp.split(mod, 6, axis=-1)
    if rows_per_group % tm == 0:
        expand = lambda a: a[:, None, :]
    else:
        expand = lambda a: jnp.repeat(a, rows_per_group, axis=0)
    sh1, sc1, gt1, sh2, sc2, gt2 = [expand(a) for a in (sh1, sc1, gt1, sh2, sc2, gt2)]
    c128, s128 = _rope_tables(pos, HEAD_DIM // 2)
    c64, s64 = _rope_tables(pos, IDX_DIM // 2)
    tabs = (c128, s128, c64, s64)
    if rows_per_group % tm != 0:
        tabs = tuple(jnp.tile(t, (tm // rows_per_group, 1)) for t in tabs)
    packed = _in_projection(x2d, sh1, sc1, w["w_in"], tabs, rows_per_group, tm)
    o_a = attend(packed)
    tci, tcc = delta_tiles
    o_b, conv_new, ssm_new = _delta_net(packed, conv0, ssm0, w["w_conv"], w["a_log"], w["dt_bias"],
                                        w["w_onorm"], groups, rows_per_group, t_real, tci, tcc, heads)
    x1, u2, logits = _out_projection(o_a, o_b, x2d, w["w_out_a"], w["w_out_b"], gt1, sh2, sc2,
                                     w["ln1_g"], w["ln1_b"], w["w_router"], w["b_router"],
                                     rows_per_group, min(tm, OUTPROJ_TM), alpha)
    x2 = _moe(u2, logits, x1, w["w_gate"], w["w_up"], w["w_down"], gt2, w["ln2_g"], w["ln2_b"],
              rows_per_group, min(tm, MOE_TM), alpha)
    return x2, packed, conv_new, ssm_new


def kernel(x_prompt, x_sample, cache_k, cache_v, cache_kidx, state_conv, state_ssm, page_table,
           c_prompt, c_sample, w_mod, b_mod, w_in, w_conv, a_log, dt_bias, w_onorm, w_out,
           ln1_g, ln1_b, w_grp, b_grp, w_erouter, b_erouter, w_gate, w_up, w_down, ln2_g, ln2_b):
    assert w_mod.shape[0] == 1, "single layer"
    batch, seq, d = x_prompt.shape
    nb, dec_seq, _ = x_sample.shape
    heads = d // (2 * HEAD_DIM)
    hw = heads * B_KEY_DIM
    past_len = page_table.shape[1] * cache_k.shape[2]
    kvw = A_KV_HEADS * HEAD_DIM

    n_rt = N_GROUPS + N_EXPERTS
    w_router = jnp.pad(jnp.concatenate([w_grp[0], w_erouter[0]], axis=1), ((0, 0), (0, LANES - n_rt)))
    b_router = jnp.pad(jnp.concatenate([b_grp[0], b_erouter[0]]), (0, LANES - n_rt)).reshape(1, LANES)
    w = dict(w_in=_pack_w_in(w_in[0], d), w_conv=w_conv[0], a_log=a_log[0], dt_bias=dt_bias[0],
             w_onorm=w_onorm[0], w_out_a=w_out[0, :hw].astype(BF16), w_out_b=w_out[0, hw:].astype(BF16),
             ln1_g=ln1_g[0], ln1_b=ln1_b[0], w_router=w_router, b_router=b_router,
             w_gate=w_gate[0].astype(BF16), w_up=w_up[0].astype(BF16), w_down=w_down[0].astype(BF16),
             ln2_g=ln2_g[0], ln2_b=ln2_b[0])

    n_c = batch + nb
    c_all = jnp.pad(jnp.concatenate([c_prompt, c_sample], axis=0), ((0, (-n_c) % SUBLANES), (0, 0)))
    mod = _modulation(c_all, w_mod[0], b_mod[0])

    def attend_prompt(packed):
        k_bf = packed[:, KA_OFF:KA_OFF + kvw].astype(BF16)
        v_bf = packed[:, VA_OFF:VA_OFF + kvw].astype(BF16)
        ki_bf = packed[:, SM_OFF + SM_KI:SM_OFF + SM_KI + IDX_DIM].astype(BF16)
        return _dsa_prompt(packed, k_bf, v_bf, ki_bf, batch, seq, heads)

    tm_p = min(INPROJ_TM, seq)
    conv0_p = jnp.zeros((batch, SUBLANES, 3 * hw), F32)
    ssm0_p = jnp.zeros((batch, heads, B_KEY_DIM, B_VAL_DIM), F32)
    tc_p = min(DELTA_TM, seq)
    y_p, packed_p, conv_p, ssm_p = _layer(
        x_prompt.reshape(batch * seq, d), mod[:batch], batch, seq, seq, jnp.arange(seq, dtype=I32), w,
        attend_prompt, conv0_p, ssm0_p, tm_p, (tc_p, tc_p))

    rows = SUBLANES
    assert CONV_WIDTH - 1 <= dec_seq <= rows
    x_s = jnp.pad(x_sample, ((0, 0), (0, rows - dec_seq), (0, 0))).reshape(nb * rows, d)
    pos_s = past_len + jnp.arange(rows, dtype=I32)

    def attend_sample(packed):
        o = _dsa_sample(packed.reshape(nb, rows, PACK_W), cache_k, cache_v, cache_kidx,
                        page_table, dec_seq, heads)
        return o.reshape(nb * rows, heads * HEAD_DIM)

    conv0_s = jnp.pad(state_conv[0], ((0, 0), (rows - (CONV_WIDTH - 1), 0), (0, 0)))
    tm_s = min(256, nb * rows)
    y_s, packed_s, conv_s, ssm_s = _layer(
        x_s, mod[batch:batch + nb], nb, rows, dec_seq, pos_s, w, attend_sample, conv0_s, state_ssm[0],
        tm_s, (rows, DELTA_CHUNK))

    def states(packed, groups, t_pad, t):
        p = packed.reshape(groups, t_pad, PACK_W)[:, :t]
        k = p[..., KA_OFF:KA_OFF + kvw].reshape(1, groups, t, A_KV_HEADS, HEAD_DIM)
        v = p[..., VA_OFF:VA_OFF + kvw].reshape(1, groups, t, A_KV_HEADS, HEAD_DIM)
        ki = p[..., SM_OFF + SM_KI:SM_OFF + SM_KI + IDX_DIM][None]
        return k, v, ki

    k_p, v_p, ki_p = states(packed_p, batch, seq, seq)
    k_s, v_s, ki_s = states(packed_s, nb, rows, dec_seq)
    conv_p = conv_p[None, :, rows - (CONV_WIDTH - 1):]
    conv_s = conv_s[None, :, rows - (CONV_WIDTH - 1):]
    y_s = y_s.reshape(nb, rows, d)[:, :dec_seq]
    return (y_p.reshape(batch, seq, d), y_s, k_p, v_p, ki_p, conv_p, ssm_p[None],
            k_s, v_s, ki_s, conv_s, ssm_s[None])
```

```python
import functools

import numpy as np
import jax
import jax.numpy as jnp
from jax import lax
from jax.experimental import pallas as pl
from jax.experimental.pallas import tpu as pltpu

F32 = jnp.float32
BF16 = jnp.bfloat16
I32 = jnp.int32

HEAD_DIM = 128
A_KV_HEADS = 2
IDX_HEADS = 16
IDX_DIM = 64
TOPK_MAX = 256
ROPE_THETA = 10000.0
B_KEY_DIM = 128
B_VAL_DIM = 128
CONV_WIDTH = 4
DELTA_CHUNK = 64
DELTA_GROUP = 4
N_GROUPS = 4
EXPERTS_PER_GROUP = 4
N_EXPERTS = N_GROUPS * EXPERTS_PER_GROUP
LN_EPS = 1e-5
RMS_EPS = 1e-6
L2_EPS = 1e-6

LANES = 128
SUBLANES = 8
VMEM_LIMIT = 56 * 1024 * 1024
INPROJ_TM = 1024
OUTPROJ_TM = 256
OUTPROJ_PARTS = 2
MOE_TM = 512
DELTA_TM = 256
DSA_TQ = 256
SAMPLE_PAGES_PER_STEP = 32

PACK_TN = 512
QA_OFF, QI_OFF, QB_OFF, KB_OFF, VB_OFF, Z_OFF, KA_OFF, VA_OFF, SM_OFF = (
    0, 1024, 2048, 3072, 4096, 5120, 6144, 6400, 6656)
PACK_W = 7168
SM_KI, SM_WI, SM_BETA, SM_DECAY = 0, 64, 80, 88
RT_GRP, RT_EXP = 0, N_GROUPS

NEG_BIG = -1e30
MASK_BIAS = -2e30
MAX_SEARCH_ITERS = 64
LOG2E = 1.4426950408889634
EXP2_SAFE_BOUND = 60.0
BOUND_SLACK = 1.02

NT_DIMS = (((1,), (1,)), ((), ()))
TN_DIMS = (((0,), (0,)), ((), ()))


def _cparams(*sem):
    return pltpu.CompilerParams(dimension_semantics=sem, vmem_limit_bytes=VMEM_LIMIT)


def _ln(x):
    mu = jnp.mean(x, axis=-1, keepdims=True)
    xc = x - mu
    return xc * lax.rsqrt(jnp.mean(xc * xc, axis=-1, keepdims=True) + LN_EPS)


def _sigmoid(x):
    return 1.0 / (1.0 + jnp.exp(-x))


def _silu(x):
    return x * _sigmoid(x)


def _dot(a, b):
    return jnp.dot(a.astype(BF16), b.astype(BF16), preferred_element_type=F32)


def _dot_nt(a, b):
    return lax.dot_general(a.astype(BF16), b.astype(BF16), NT_DIMS, preferred_element_type=F32)


def _split(a):
    hi = a.astype(BF16)
    lo = (a - hi.astype(F32)).astype(BF16)
    return hi, lo


def _dot3(a, b):
    ah, al = _split(a)
    bh, bl = _split(b)
    d = functools.partial(jnp.dot, preferred_element_type=F32)
    return d(ah, bh) + (d(ah, bl) + d(al, bh))


def _kth_threshold(count_ge, mn, mx, n_adm, topk):
    k = float(topk)
    all_in = n_adm <= k
    done0 = jnp.where(all_in | (count_ge(mx) >= k), 1.0, 0.0)
    thr0 = jnp.where(all_in, mn, mx)

    def cond(st):
        return (st[0] < MAX_SEARCH_ITERS) & (st[-1] > 0.0)

    def body(st):
        it, lo, hi, thr, done_f, _ = st
        done = done_f > 0.5
        mid = 0.5 * lo + 0.5 * hi
        stuck = jnp.logical_not((mid > lo) & (mid < hi))
        c = count_ge(mid)
        ge = c >= k
        hit = c == k
        thr = jnp.where(done, thr, jnp.where(hit, mid, jnp.where(stuck, lo, thr)))
        done_f = jnp.where(done | hit | stuck, 1.0, 0.0)
        return it + 1, jnp.where(ge, mid, lo), jnp.where(ge, hi, mid), thr, done_f, jnp.sum(1.0 - done_f)

    st = lax.while_loop(cond, body, (jnp.int32(0), mn, mx, thr0, done0, jnp.sum(1.0 - done0)))
    return jnp.where(st[4] > 0.5, st[3], st[1])


def _mod_kernel(c_ref, w_ref, b_ref, o_ref):
    o_ref[...] = _dot(_silu(c_ref[...]), w_ref[...]) + b_ref[...]


def _modulation(c, w_mod, b_mod):
    m, d = c.shape
    n = w_mod.shape[1]
    tn = 1024
    return pl.pallas_call(
        _mod_kernel,
        out_shape=jax.ShapeDtypeStruct((m, n), F32),
        grid=(n // tn,),
        in_specs=[pl.BlockSpec((m, d), lambda j: (0, 0)),
                  pl.BlockSpec((d, tn), lambda j: (0, j)),
                  pl.BlockSpec((1, tn), lambda j: (0, j))],
        out_specs=pl.BlockSpec((m, tn), lambda j: (0, j)),
        compiler_params=_cparams("arbitrary"),
    )(c, w_mod, b_mod.reshape(1, n))


def _rope_tables(pos, half):
    inv = ROPE_THETA ** (-jnp.arange(half, dtype=F32) / half)
    ang = pos.astype(F32)[:, None] * inv[None, :]
    c, s = jnp.cos(ang), jnp.sin(ang)
    rep = LANES // (2 * half)
    cos = jnp.tile(jnp.concatenate([c, c], -1), (1, rep))
    sin = jnp.tile(jnp.concatenate([-s, s], -1), (1, rep))
    return cos, sin


def _rope128(x, cos, sin):
    return x * cos + pltpu.roll(x, HEAD_DIM // 2, 1) * sin


def _rope64(x, cos, sin):
    lane = lax.broadcasted_iota(I32, x.shape, 1)
    first = (lane % IDX_DIM) < (IDX_DIM // 2)
    rot = jnp.where(first, pltpu.roll(x, LANES - IDX_DIM // 2, 1), pltpu.roll(x, IDX_DIM // 2, 1))
    return x * cos + rot * sin


def _inproj_kernel(x_ref, sh_ref, sc_ref, w_ref, c128_ref, s128_ref, c64_ref, s64_ref,
                   o_ref, u_ref, *, wi_scale):
    n = pl.program_id(1)

    @pl.when(n == 0)
    def _():
        y = _ln(x_ref[...]) * (1.0 + sc_ref[...]) + sh_ref[...]
        u_ref[...] = y.astype(BF16)

    acc = jnp.dot(u_ref[...], w_ref[...], preferred_element_type=F32)

    def kind(col):
        if col < QI_OFF or KA_OFF <= col < VA_OFF:
            return "rope128"
        if col < QB_OFF:
            return "rope64"
        return "small" if col == SM_OFF else "plain"

    def store(j, k):
        sl = slice(j * LANES, (j + 1) * LANES)
        a = acc[:, sl]
        if k == "rope128":
            a = _rope128(a, c128_ref[...], s128_ref[...])
        elif k == "rope64":
            a = _rope64(a, c64_ref[...], s64_ref[...])
        elif k == "small":
            lane = lax.broadcasted_iota(I32, a.shape, 1)
            r = _rope64(a, c64_ref[...], s64_ref[...])
            a = jnp.where(lane < SM_WI, r, jnp.where(lane < SM_BETA, a * wi_scale, a))
        o_ref[:, sl] = a

    sub = PACK_TN // LANES
    tile_kinds = [tuple(kind(t * PACK_TN + j * LANES) for j in range(sub)) for t in range(PACK_W // PACK_TN)]
    plain = ("plain",) * sub
    for kinds in sorted(set(tile_kinds) - {plain}):
        tiles = [t for t, k in enumerate(tile_kinds) if k == kinds]
        cond = functools.reduce(jnp.logical_or, [n == t for t in tiles])

        @pl.when(cond)
        def _(kinds=kinds):
            for j, k in enumerate(kinds):
                store(j, k)

    @pl.when(functools.reduce(jnp.logical_and, [n != t for t, k in enumerate(tile_kinds) if k != plain]))
    def _():
        o_ref[...] = acc


def _in_projection(x, sh, sc, w_pack, tabs, rows_per_group, tm):
    m, d = x.shape
    c128, s128, c64, s64 = tabs
    nt = PACK_W // PACK_TN
    if sh.ndim == 3:
        tpg = rows_per_group // tm
        mod_spec = pl.BlockSpec((None, 1, d), lambda i, n: (i // tpg, 0, 0))
    else:
        mod_spec = pl.BlockSpec((tm, d), lambda i, n: (i, 0))
    tab_blocks = c128.shape[0] // tm
    tab_spec = pl.BlockSpec((tm, LANES), lambda i, n: (i % tab_blocks, 0))
    wi_scale = float((IDX_HEADS * IDX_DIM) ** -0.5)
    return pl.pallas_call(
        functools.partial(_inproj_kernel, wi_scale=wi_scale),
        out_shape=jax.ShapeDtypeStruct((m, PACK_W), F32),
        grid=(m // tm, nt),
        in_specs=[pl.BlockSpec((tm, d), lambda i, n: (i, 0)), mod_spec, mod_spec,
                  pl.BlockSpec((None, d, PACK_TN), lambda i, n: (n, 0, 0)),
                  tab_spec, tab_spec, tab_spec, tab_spec],
        out_specs=pl.BlockSpec((tm, PACK_TN), lambda i, n: (i, n)),
        scratch_shapes=[pltpu.VMEM((tm, d), BF16)],
        compiler_params=_cparams("arbitrary", "arbitrary"),
    )(x, sh, sc, w_pack, c128, s128, c64, s64)


def _pack_w_in(w_in, d):
    heads = d // (2 * HEAD_DIM)
    conv_dim = 2 * heads * B_KEY_DIM + heads * B_VAL_DIM
    sizes = (heads * HEAD_DIM, A_KV_HEADS * HEAD_DIM, A_KV_HEADS * HEAD_DIM, IDX_HEADS * IDX_DIM,
             IDX_DIM, IDX_HEADS, conv_dim, heads * B_VAL_DIM, heads, heads)
    pts = [int(v) for v in np.cumsum(sizes)[:-1]]
    qa, ka, va, qi, ki, wi, qkv, z, braw, araw = jnp.split(w_in, pts, axis=1)
    small = jnp.concatenate([ki, wi, braw, araw], axis=1)
    small = jnp.pad(small, ((0, 0), (0, PACK_W - SM_OFF - small.shape[1])))
    w = jnp.concatenate([qa, qi, qkv, z, ka, va, small], axis=1)
    assert w.shape[1] == PACK_W
    return w.astype(BF16).reshape(d, PACK_W // PACK_TN, PACK_TN).transpose(1, 0, 2)


def _dsa_prompt_kernel(q_ref, qi_ref, sm_ref, k_ref, vt_ref, ki_ref, o_ref,
                       qt_ref, qit_ref, w_ref, sc_ref, s_ref, p_ref, m_ref, l_ref, a_ref, acc_ref,
                       b_ref, kmax_ref, lp_ref, *, tq, nq, topk, heads):
    i = pl.program_id(1)
    nkb = i + 1
    tk = tq
    scale = float(HEAD_DIM ** -0.5 * LOG2E)
    hpg = heads // A_KV_HEADS

    @pl.when(i == 0)
    def _():
        def body(kb, mxs):
            x = k_ref[pl.ds(pl.multiple_of(kb * tk, tk), tk), :].astype(F32)
            x2 = x * x
            return tuple(jnp.maximum(mxs[g], jnp.sum(x2[:, g * HEAD_DIM:(g + 1) * HEAD_DIM], axis=1, keepdims=True))
                         for g in range(A_KV_HEADS))
        mxs = lax.fori_loop(0, nq, body, (jnp.zeros((tk, 1), F32),) * A_KV_HEADS)
        for g in range(A_KV_HEADS):
            kmax_ref[g] = jnp.broadcast_to(jnp.sqrt(jnp.max(mxs[g], axis=0, keepdims=True)), kmax_ref.shape[1:])

    for h in range(heads):
        qt = jnp.transpose(q_ref[:, h * HEAD_DIM:(h + 1) * HEAD_DIM] * scale)
        qt_ref[h] = qt.astype(BF16)
        b_ref[h] = (jnp.sqrt(jnp.sum(qt * qt, axis=0, keepdims=True)) * kmax_ref[h // hpg, 0:1, :]) * BOUND_SLACK
    for hp in range(IDX_HEADS * IDX_DIM // LANES):
        t = jnp.transpose(qi_ref[:, hp * LANES:(hp + 1) * LANES])
        for u in range(LANES // IDX_DIM):
            qit_ref[hp * (LANES // IDX_DIM) + u] = t[u * IDX_DIM:(u + 1) * IDX_DIM].astype(BF16)
    w_ref[...] = jnp.transpose(sm_ref[...])[SM_WI:SM_WI + IDX_HEADS]
    qcol = i * tq + lax.broadcasted_iota(I32, (tk, tq), 1)
    krow = lax.broadcasted_iota(I32, (tk, tq), 0)

    def scores(kb, carry):
        mn, mx = carry
        off = pl.multiple_of(kb * tk, tk)
        kblk = ki_ref[pl.ds(off, tk), :]
        acc = jnp.zeros((tk, tq), F32)
        for h in range(IDX_HEADS):
            s = jnp.dot(kblk, qit_ref[h], preferred_element_type=F32)
            acc = acc + jnp.maximum(s, 0.0) * w_ref[h:h + 1, :]
        adm = krow + kb * tk <= qcol
        sc_ref[kb] = jnp.where(adm, acc, -jnp.inf)
        mn = jnp.minimum(mn, jnp.min(jnp.where(adm, acc, jnp.inf), axis=0, keepdims=True))
        mx = jnp.maximum(mx, jnp.max(jnp.where(adm, acc, -jnp.inf), axis=0, keepdims=True))
        return mn, mx

    mn, mx = lax.fori_loop(0, nkb, scores,
                           (jnp.full((1, tq), jnp.inf, F32), jnp.full((1, tq), -jnp.inf, F32)))

    sc_ref[nkb] = jnp.full((tk, tq), -jnp.inf, F32)

    def count_ge(t):
        def body(pi, cnt):
            for u in range(2):
                ind = jnp.where(sc_ref[2 * pi + u] >= t, 1.0, 0.0)
                cnt = cnt + jnp.sum(ind.reshape(tk // SUBLANES, SUBLANES, tq), axis=0)
            return cnt
        cnt = lax.fori_loop(0, (nkb + 1) // 2, body, jnp.zeros((SUBLANES, tq), F32))
        return jnp.sum(cnt, axis=0, keepdims=True)

    n_adm = (i * tq + lax.broadcasted_iota(I32, (1, tq), 1) + 1).astype(F32)
    thr = _kth_threshold(count_ge, mn, mx, n_adm, topk)
    acc_ref[...] = jnp.zeros(acc_ref.shape, F32)
    bmax = jnp.max(jnp.concatenate([b_ref[h] for h in range(heads)], axis=0))

    def keys_of(kb, h):
        g = h // hpg
        return k_ref[pl.ds(pl.multiple_of(kb * tk, tk), tk), g * HEAD_DIM:(g + 1) * HEAD_DIM]

    @pl.when(bmax <= EXP2_SAFE_BOUND)
    def _():
        lp_ref[...] = jnp.zeros(lp_ref.shape, F32)

        def attend(kb, c):
            bias = jnp.where(sc_ref[kb] >= thr, 0.0, MASK_BIAS)
            for h in range(heads):
                p = jnp.exp2(jnp.dot(keys_of(kb, h), qt_ref[h], preferred_element_type=F32) + (bias - b_ref[h]))
                lp_ref[h] += jnp.sum(p.reshape(tk // SUBLANES, SUBLANES, tq), axis=0)
                p_ref[h] = p.astype(BF16)
            for h in range(heads):
                acc_ref[h] += jnp.dot(vt_ref[kb, h // hpg], p_ref[h], preferred_element_type=F32)
            return c

        lax.fori_loop(0, nkb, attend, 0)
        for h in range(heads):
            l = jnp.sum(lp_ref[h], axis=0, keepdims=True)
            o_ref[:, h * HEAD_DIM:(h + 1) * HEAD_DIM] = jnp.transpose(acc_ref[h] / l).astype(BF16)

    @pl.when(bmax > EXP2_SAFE_BOUND)
    def _():
        m_ref[...] = jnp.full(m_ref.shape, NEG_BIG, F32)
        l_ref[...] = jnp.zeros(l_ref.shape, F32)

        def attend(kb, c):
            bias = jnp.where(sc_ref[kb] >= thr, 0.0, MASK_BIAS)
            for h in range(heads):
                s_ref[h] = jnp.dot(keys_of(kb, h), qt_ref[h], preferred_element_type=F32) + bias
            for h in range(heads):
                s = s_ref[h]
                m_old = m_ref[h]
                m_new = jnp.maximum(m_old, jnp.max(s, axis=0, keepdims=True))
                p = jnp.exp2(s - m_new)
                alpha = jnp.exp2(m_old - m_new)
                l_ref[h] = alpha * l_ref[h] + jnp.sum(p, axis=0, keepdims=True)
                p_ref[h] = p.astype(BF16)
                m_ref[h] = m_new
                a_ref[h] = alpha
            for h in range(heads):
                acc_ref[h] = a_ref[h] * acc_ref[h] + jnp.dot(vt_ref[kb, h // hpg], p_ref[h],
                                                             preferred_element_type=F32)
            return c

        lax.fori_loop(0, nkb, attend, 0)
        for h in range(heads):
            o_ref[:, h * HEAD_DIM:(h + 1) * HEAD_DIM] = jnp.transpose(acc_ref[h] / l_ref[h]).astype(BF16)


def _dsa_prompt(packed, k_bf, v_bf, ki_bf, batch, seq, heads):
    tq = min(DSA_TQ, seq)
    nq = seq // tq
    topk = min(TOPK_MAX, seq // 4)
    aw = heads * HEAD_DIM
    iw = IDX_HEADS * IDX_DIM
    kvw = A_KV_HEADS * HEAD_DIM
    vt = v_bf.reshape(batch, nq, tq, A_KV_HEADS, HEAD_DIM).transpose(0, 1, 3, 4, 2)
    return pl.pallas_call(
        functools.partial(_dsa_prompt_kernel, tq=tq, nq=nq, topk=topk, heads=heads),
        out_shape=jax.ShapeDtypeStruct((batch * seq, aw), BF16),
        grid=(batch, nq),
        in_specs=[pl.BlockSpec((tq, aw), lambda b, i: (b * nq + i, QA_OFF // aw)),
                  pl.BlockSpec((tq, iw), lambda b, i: (b * nq + i, QI_OFF // iw)),
                  pl.BlockSpec((tq, LANES), lambda b, i: (b * nq + i, SM_OFF // LANES)),
                  pl.BlockSpec((seq, kvw), lambda b, i: (b, 0)),
                  pl.BlockSpec((None, nq, A_KV_HEADS, HEAD_DIM, tq), lambda b, i: (b, 0, 0, 0, 0)),
                  pl.BlockSpec((seq, IDX_DIM), lambda b, i: (b, 0))],
        out_specs=pl.BlockSpec((tq, aw), lambda b, i: (b * nq + i, 0)),
        scratch_shapes=[pltpu.VMEM((heads, HEAD_DIM, tq), BF16),
                        pltpu.VMEM((IDX_HEADS, IDX_DIM, tq), BF16),
                        pltpu.VMEM((IDX_HEADS, tq), F32),
                        pltpu.VMEM((nq + 1, tq, tq), F32),
                        pltpu.VMEM((heads, tq, tq), F32),
                        pltpu.VMEM((heads, tq, tq), BF16),
                        pltpu.VMEM((heads, 1, tq), F32),
                        pltpu.VMEM((heads, 1, tq), F32),
                        pltpu.VMEM((heads, 1, tq), F32),
                        pltpu.VMEM((heads, HEAD_DIM, tq), F32),
                        pltpu.VMEM((heads, 1, tq), F32),
                        pltpu.VMEM((A_KV_HEADS, SUBLANES, tq), F32),
                        pltpu.VMEM((heads, SUBLANES, tq), F32)],
        compiler_params=_cparams("arbitrary", "arbitrary"),
    )(packed, packed, packed, k_bf, vt, ki_bf)


def _dsa_sample_kernel(pt_ref, q_ref, qi_ref, sm_ref, kn_ref, vn_ref, *rest,
                       scoring, pg, npg, past_len, t_real, topk, heads):
    del pt_ref
    if scoring:
        kidx_refs = rest[:pg]
        sc_ref, thr_ref, qi2_ref, w_ref, mm_ref, pad_ref = rest[pg:]
    else:
        sc_ref, thr_ref = rest[:2]
        k_refs = rest[2:2 + pg]
        v_refs = rest[2 + pg:2 + 2 * pg]
        o_ref, qt_ref, pad_ref, m_ref, l_ref, acc_ref = rest[2 + 2 * pg:]
    j = pl.program_id(1)
    rows = SUBLANES
    ps = LANES
    scale = float(HEAD_DIM ** -0.5)
    hpg = heads // A_KV_HEADS

    def indexer(kt):
        n = kt.shape[1]
        s = jnp.dot(qi2_ref[...], kt.astype(BF16), preferred_element_type=F32)
        x = jnp.maximum(s, 0.0) * jnp.tile(w_ref[...], (1, n // LANES))
        return jnp.sum(x.reshape(IDX_HEADS, rows, n), axis=0)

    def lane_form(x):
        return jnp.transpose(jnp.tile(x, (LANES // rows, 1)))

    def fold_lanes(x, op):
        r = x[:, 0:LANES]
        for t in range(1, x.shape[1] // LANES):
            r = op(r, x[:, t * LANES:(t + 1) * LANES])
        return r

    def padded(block):
        pad_ref[...] = jnp.zeros(pad_ref.shape, F32)
        pad_ref[0:rows, 0:block.shape[1]] = block
        return pad_ref[:, 0:block.shape[1]]

    def softmax_step(s, v_of_group):
        m_old = m_ref[0:1, :]
        m_new = jnp.maximum(m_old, jnp.max(s, axis=0, keepdims=True))
        p = jnp.exp(s - m_new).astype(BF16)
        alpha = jnp.exp(m_old - m_new)
        l_ref[...] = jnp.broadcast_to(alpha * l_ref[0:1, :] + jnp.sum(p.astype(F32), axis=0, keepdims=True),
                                      l_ref.shape)
        m_ref[...] = jnp.broadcast_to(m_new, m_ref.shape)
        for g in range(A_KV_HEADS):
            acc_ref[g] = alpha * acc_ref[g] + lax.dot_general(
                v_of_group(g).astype(BF16), p, TN_DIMS, preferred_element_type=F32)

    @pl.when(j == 0)
    def _():
        if scoring:
            for h in range(IDX_HEADS):
                qi2_ref[h * rows:(h + 1) * rows, :] = qi_ref[:, h * IDX_DIM:(h + 1) * IDX_DIM].astype(BF16)
                w_ref[h * rows:(h + 1) * rows, :] = jnp.broadcast_to(
                    sm_ref[:, SM_WI + h:SM_WI + h + 1], (rows, LANES))
            mm_ref[0] = jnp.full((rows, LANES), jnp.inf, F32)
            mm_ref[1] = jnp.full((rows, LANES), -jnp.inf, F32)
        else:
            for g in range(A_KV_HEADS):
                pad_ref[...] = jnp.zeros(pad_ref.shape, F32)
                for h in range(g * hpg, (g + 1) * hpg):
                    pad_ref[h * rows:(h + 1) * rows, :] = q_ref[:, h * HEAD_DIM:(h + 1) * HEAD_DIM] * scale
                qt_ref[g] = jnp.transpose(pad_ref[...]).astype(BF16)
            m_ref[...] = jnp.full(m_ref.shape, NEG_BIG, F32)
            l_ref[...] = jnp.zeros(l_ref.shape, F32)
            acc_ref[...] = jnp.zeros(acc_ref.shape, F32)

    if scoring:
        x = indexer(jnp.concatenate([r[...] for r in kidx_refs], axis=1))
        sc_ref[j] = x
        mm_ref[0] = jnp.minimum(mm_ref[0], fold_lanes(x, jnp.minimum))
        mm_ref[1] = jnp.maximum(mm_ref[1], fold_lanes(x, jnp.maximum))

    def finish_scores():
        kt_new = jnp.transpose(padded(sm_ref[...]))[SM_KI:SM_KI + IDX_DIM]
        x = indexer(kt_new)
        qrow = lax.broadcasted_iota(I32, (rows, ps), 0)
        kcol = lax.broadcasted_iota(I32, (rows, ps), 1)
        adm = (kcol <= qrow) & (kcol < t_real)
        sc_ref[npg] = jnp.full(sc_ref.shape[1:], -jnp.inf, F32)
        sc_ref[npg, :, 0:ps] = jnp.where(adm, x, -jnp.inf)
        mn = jnp.min(jnp.minimum(mm_ref[0], jnp.where(adm, x, jnp.inf)), axis=1, keepdims=True)
        mx = jnp.max(jnp.maximum(mm_ref[1], jnp.where(adm, x, -jnp.inf)), axis=1, keepdims=True)

        def count_ge(t):
            def body(b, cnt):
                return cnt + fold_lanes(jnp.where(sc_ref[b] >= t, 1.0, 0.0), jnp.add)
            cnt = lax.fori_loop(0, npg + 1, body, jnp.zeros((rows, LANES), F32))
            return jnp.sum(cnt, axis=1, keepdims=True)

        q1 = lax.broadcasted_iota(I32, (rows, 1), 0)
        n_adm = (past_len + jnp.minimum(q1, t_real - 1) + 1).astype(F32)
        thr = _kth_threshold(count_ge, mn, mx, n_adm, topk)
        thr_ref[...] = lane_form(jnp.broadcast_to(thr, (rows, LANES)))[0:rows]

    if scoring:
        pl.when(j == npg - 1)(finish_scores)
        return

    s = jnp.where(lane_form(sc_ref[j]) >= thr_ref[0:1, :], 0.0, MASK_BIAS)
    for g in range(A_KV_HEADS):
        kg = jnp.concatenate([r[pl.ds(g, ps, stride=A_KV_HEADS), :] for r in k_refs], axis=0)
        s = s + jnp.dot(kg.astype(BF16), qt_ref[g], preferred_element_type=F32)
    softmax_step(s, lambda g: jnp.concatenate([r[pl.ds(g, ps, stride=A_KV_HEADS), :] for r in v_refs], axis=0))

    @pl.when(j == npg - 1)
    def _():
        s = jnp.where(lane_form(sc_ref[npg, :, 0:ps]) >= thr_ref[0:1, :], 0.0, MASK_BIAS)
        for g in range(A_KV_HEADS):
            kg = padded(kn_ref[:, g * HEAD_DIM:(g + 1) * HEAD_DIM])
            s = s + jnp.dot(kg.astype(BF16), qt_ref[g], preferred_element_type=F32)
        softmax_step(s, lambda g: padded(vn_ref[:, g * HEAD_DIM:(g + 1) * HEAD_DIM]))
        for g in range(A_KV_HEADS):
            out = jnp.transpose(acc_ref[g] / l_ref[0:1, :])
            for h in range(g * hpg, (g + 1) * hpg):
                o_ref[:, h * HEAD_DIM:(h + 1) * HEAD_DIM] = out[h * rows:(h + 1) * rows, :].astype(BF16)


def _dsa_sample(packed3, cache_k, cache_v, cache_kidx, page_table, t_real, heads):
    nb, rows, _ = packed3.shape
    assert rows == SUBLANES and IDX_HEADS * rows == LANES
    ps = cache_k.shape[2]
    assert ps == LANES and cache_k.shape[0] == 1
    n_pages = page_table.shape[1]
    pg = min(SAMPLE_PAGES_PER_STEP, n_pages)
    npg = n_pages // pg
    assert n_pages % pg == 0
    kvw = A_KV_HEADS * HEAD_DIM
    aw = heads * HEAD_DIM
    iw = IDX_HEADS * IDX_DIM
    past_len = n_pages * ps
    topk = min(TOPK_MAX, (past_len + t_real) // 4)
    pt = page_table.reshape(-1).astype(I32)

    def page(b, j, pt, i):
        return pt[b * n_pages + j * pg + i]

    kidx_t = jnp.swapaxes(cache_kidx, 2, 3)

    def kidx_spec(i):
        return pl.BlockSpec((None, None, IDX_DIM, ps), lambda b, j, pt: (0, page(b, j, pt, i), 0, 0))

    ck = cache_k.reshape(cache_k.shape[1], ps * A_KV_HEADS, HEAD_DIM)
    cv = cache_v.reshape(cache_v.shape[1], ps * A_KV_HEADS, HEAD_DIM)

    def kv_spec(i):
        return pl.BlockSpec((None, ps * A_KV_HEADS, HEAD_DIM),
                            lambda b, j, pt: (page(b, j, pt, i), 0, 0))

    row_specs = [pl.BlockSpec((None, rows, aw), lambda b, j, pt: (b, 0, QA_OFF // aw)),
                 pl.BlockSpec((None, rows, iw), lambda b, j, pt: (b, 0, QI_OFF // iw)),
                 pl.BlockSpec((None, rows, LANES), lambda b, j, pt: (b, 0, SM_OFF // LANES)),
                 pl.BlockSpec((None, rows, kvw), lambda b, j, pt: (b, 0, KA_OFF // kvw)),
                 pl.BlockSpec((None, rows, kvw), lambda b, j, pt: (b, 0, VA_OFF // kvw))]
    sc_spec = pl.BlockSpec((None, npg + 1, rows, pg * ps), lambda b, j, pt: (b, 0, 0, 0))
    thr_spec = pl.BlockSpec((None, rows, LANES), lambda b, j, pt: (b, 0, 0))
    params = dict(pg=pg, npg=npg, past_len=past_len, t_real=t_real, topk=topk, heads=heads)
    rows5 = (packed3,) * 5
    sc, thr = pl.pallas_call(
        functools.partial(_dsa_sample_kernel, scoring=True, **params),
        out_shape=(jax.ShapeDtypeStruct((nb, npg + 1, rows, pg * ps), F32),
                   jax.ShapeDtypeStruct((nb, rows, LANES), F32)),
        grid_spec=pltpu.PrefetchScalarGridSpec(
            num_scalar_prefetch=1,
            grid=(nb, npg),
            in_specs=row_specs + [kidx_spec(i) for i in range(pg)],
            out_specs=(sc_spec, thr_spec),
            scratch_shapes=[pltpu.VMEM((IDX_HEADS * rows, IDX_DIM), BF16),
                            pltpu.VMEM((IDX_HEADS * rows, LANES), F32),
                            pltpu.VMEM((2, rows, LANES), F32),
                            pltpu.VMEM((LANES, LANES), F32)]),
        compiler_params=_cparams("arbitrary", "arbitrary"),
    )(pt, *rows5, *([kidx_t] * pg))
    return pl.pallas_call(
        functools.partial(_dsa_sample_kernel, scoring=False, **params),
        out_shape=jax.ShapeDtypeStruct((nb, rows, aw), BF16),
        grid_spec=pltpu.PrefetchScalarGridSpec(
            num_scalar_prefetch=1,
            grid=(nb, npg),
            in_specs=row_specs + [sc_spec, thr_spec] + [kv_spec(i) for i in range(pg)] * 2,
            out_specs=pl.BlockSpec((None, rows, aw), lambda b, j, pt: (b, 0, 0)),
            scratch_shapes=[pltpu.VMEM((A_KV_HEADS, HEAD_DIM, LANES), BF16),
                            pltpu.VMEM((LANES, LANES), F32),
                            pltpu.VMEM((rows, LANES), F32),
                            pltpu.VMEM((rows, LANES), F32),
                            pltpu.VMEM((A_KV_HEADS, HEAD_DIM, LANES), F32)]),
        compiler_params=_cparams("arbitrary", "arbitrary"),
    )(pt, *rows5, sc, thr, *([ck] * pg), *([cv] * pg))


def _split3(x):
    x1 = x.astype(BF16)
    r = x - x1.astype(F32)
    x2 = r.astype(BF16)
    return x1, x2, (r - x2.astype(F32)).astype(BF16)


def _exact_right(x, m01):
    d = functools.partial(jnp.dot, preferred_element_type=F32)
    x1, x2, x3 = _split3(x)
    return (d(x1, m01) + d(x2, m01)) + d(x3, m01)


def _exact_left(m01, x):
    d = functools.partial(jnp.dot, preferred_element_type=F32)
    x1, x2, x3 = _split3(x)
    return (d(m01, x1) + d(m01, x2)) + d(m01, x3)


def _dot3s(ah, al, bh, bl):
    d = functools.partial(jnp.dot, preferred_element_type=F32)
    return d(ah, bh) + (d(ah, bl) + d(al, bh))


def _blockdiag_mask(nb, rb, cb):
    shape = (nb * rb, nb * cb)
    return lax.broadcasted_iota(I32, shape, 0) // rb == lax.broadcasted_iota(I32, shape, 1) // cb


def _blockdiag(x, mask):
    t = jnp.tile(x, (mask.shape[0] // x.shape[0], 1))
    return jnp.where(mask, t, jnp.zeros_like(t))


def _unit_lower_inverse_cat(a_cats, c, mask):
    shape = a_cats[0].shape
    eye = (lax.broadcasted_iota(I32, shape, 0) == lax.broadcasted_iota(I32, shape, 1) % c).astype(F32)

    def bd_halves(p):
        ph, pl_ = _split(p)
        return ph, pl_, _blockdiag(ph, mask), _blockdiag(pl_, mask)

    xs = [eye - a for a in a_cats]
    ps = []
    for a in a_cats:
        ph, pl_, bh, bl = bd_halves(a)
        ps.append(_dot3s(ph, pl_, bh, bl))
    n = 2
    while n < c:
        nxt_x, nxt_p = [], []
        for x, p in zip(xs, ps):
            _, _, bh, bl = bd_halves(p)
            if 2 * n < c:
                lh, ll = _split(jnp.concatenate([x, p], axis=0))
                r = _dot3s(lh, ll, bh, bl)
                nxt_x.append(x + r[0:c])
                nxt_p.append(r[c:2 * c])
            else:
                xh, xl = _split(x)
                nxt_x.append(x + _dot3s(xh, xl, bh, bl))
                nxt_p.append(p)
        xs, ps = nxt_x, nxt_p
        n *= 2
    d = functools.partial(jnp.dot, preferred_element_type=F32)
    res = []
    for a, x in zip(a_cats, xs):
        t1, t2, t3 = _split3(eye + a)
        x1, x2, x3 = (_blockdiag(v, mask) for v in _split3(x))
        tx = ((d(t1, x1) + d(t1, x2)) + (d(t2, x1) + d(t1, x3))) + (d(t2, x2) + d(t3, x1))
        res.append(eye - tx)
    out = []
    for x, r in zip(xs, res):
        xh, xl = _split(x)
        rh, rl = _split(r)
        out.append(x + _dot3s(xh, xl, _blockdiag(rh, mask), _blockdiag(rl, mask)))
    return out


def _delta_kernel(qb_ref, kb_ref, vb_ref, z_ref, sm_ref, conv0_ref, wconv_ref, alog_ref, dtb_ref,
                  wnorm_ref, ssm0_ref, eb_ref, eg_ref, egc_ref, o_ref, convo_ref, ssmo_ref, xbuf_ref, s_ref,
                  *, tci, tcc, t_real, heads):
    j = pl.program_id(1)
    c = DELTA_CHUNK
    hw = heads * B_KEY_DIM
    halo = SUBLANES
    hg = DELTA_GROUP
    gw, cw, pw = hg * B_KEY_DIM, hg * c, 2 * B_KEY_DIM

    @pl.when(j == 0)
    def _():
        xbuf_ref[0:halo, :] = conv0_ref[...]
        s_ref[...] = jnp.zeros(s_ref.shape, F32)
        for h in range(heads):
            o = (h % 2) * B_KEY_DIM
            s_ref[h // 2, o:o + B_KEY_DIM, o:o + B_VAL_DIM] = ssm0_ref[h]

    @pl.when(j > 0)
    def _():
        xbuf_ref[0:halo, :] = xbuf_ref[tcc:tcc + halo, :]

    xbuf_ref[halo:halo + tci, 0:hw] = qb_ref[...]
    xbuf_ref[halo:halo + tci, hw:2 * hw] = kb_ref[...]
    xbuf_ref[halo:halo + tci, 2 * hw:3 * hw] = vb_ref[...]
    if tci < tcc:
        xbuf_ref[halo + tci:halo + tcc, :] = jnp.zeros((tcc - tci, 3 * hw), F32)

    jr, r1 = (t_real - 1) // tcc, t_real - ((t_real - 1) // tcc) * tcc

    @pl.when(j == jr)
    def _():
        convo_ref[...] = jnp.zeros(convo_ref.shape, F32)
        convo_ref[halo - (CONV_WIDTH - 1):halo, :] = xbuf_ref[halo + r1 - (CONV_WIDTH - 1):halo + r1, :]

    conv = xbuf_ref[halo - 3:halo - 3 + tcc, :] * wconv_ref[0:1, :]
    for t in range(1, CONV_WIDTH):
        conv = conv + xbuf_ref[halo - 3 + t:halo - 3 + t + tcc, :] * wconv_ref[t:t + 1, :]
    act = _silu(conv)

    sm = sm_ref[...]
    if tci < tcc:
        sm = jnp.concatenate([sm, jnp.zeros((tcc - tci, LANES), F32)], axis=0)
    lane = lax.broadcasted_iota(I32, (tcc, LANES), 1)
    trow = j * tcc + lax.broadcasted_iota(I32, (tcc, LANES), 0)
    valid = trow < t_real
    beta_t = jnp.where(valid & (lane >= SM_BETA) & (lane < SM_BETA + heads), _sigmoid(sm), 0.0)
    xg = sm + dtb_ref[...]
    softplus = jnp.maximum(xg, 0.0) + jnp.log1p(jnp.exp(-jnp.abs(xg)))
    lg_t = jnp.where(valid & (lane >= SM_DECAY) & (lane < SM_DECAY + heads),
                     -jnp.exp(alog_ref[...]) * softplus, 0.0)
    ri = lax.broadcasted_iota(I32, (tcc, tcc), 0)
    ci = lax.broadcasted_iota(I32, (tcc, tcc), 1)
    tri = ((ci <= ri) & (ci // c == ri // c)).astype(BF16)
    g_t = _exact_left(tri, lg_t)
    rmask = valid[:, 0:1]

    def l2norm_heads(x, scale):
        parts = []
        for h in range(heads):
            xh = x[:, h * B_KEY_DIM:(h + 1) * B_KEY_DIM]
            parts.append(xh * (lax.rsqrt(jnp.sum(xh * xh, axis=-1, keepdims=True) + L2_EPS) * scale))
        return jnp.concatenate(parts, axis=1)

    qn_all = l2norm_heads(act[:, 0:hw], float(B_KEY_DIM ** -0.5))
    kn_all = l2norm_heads(act[:, hw:2 * hw], 1.0)
    v_all = act[:, 2 * hw:3 * hw]
    if t_real % tcc != 0 or tci < tcc:
        kn_all = jnp.where(rmask, kn_all, 0.0)
        v_all = jnp.where(rmask, v_all, 0.0)
    beta_e = _exact_right(beta_t, eb_ref[...])
    g_e = _exact_right(g_t, eg_ref[...])
    gc_all = _exact_right(g_t, egc_ref[...])
    eg_e = jnp.exp(g_e)
    kbeta_all = kn_all * beta_e
    vbeta_all = v_all * beta_e
    kbe_all = kbeta_all * eg_e
    qe_all = qn_all * eg_e

    crow = lax.broadcasted_iota(I32, (c, heads * c), 0)
    ccol = lax.broadcasted_iota(I32, (c, heads * c), 1) % c
    strict = (lax.broadcasted_iota(I32, (c, cw), 0) > lax.broadcasted_iota(I32, (c, cw), 1) % c)
    ones_cc = jnp.ones((c, c), BF16)
    assert B_KEY_DIM == B_VAL_DIM
    mask_c = _blockdiag_mask(hg, c, c)
    mask_d = _blockdiag_mask(hg, c, B_KEY_DIM)
    prow = lax.broadcasted_iota(I32, (pw, pw), 0) // B_KEY_DIM
    pcol = lax.broadcasted_iota(I32, (pw, pw), 1) // B_VAL_DIM
    units = [(ch, grp) for ch in range(tcc // c) for grp in range(heads // hg)]
    a_cats, qk_cats = [], []
    for ch in range(tcc // c):
        r = slice(ch * c, (ch + 1) * c)
        gc = gc_all[r]
        grow = _exact_left(ones_cc, jnp.where(crow == ccol, gc, 0.0))
        decay = jnp.where(crow >= ccol, jnp.exp(jnp.minimum(gc - grow, 0.0)), 0.0)
        for grp in range(heads // hg):
            ls = slice(grp * gw, (grp + 1) * gw)
            dec = decay[:, grp * cw:(grp + 1) * cw]
            kn_bd = _blockdiag(kn_all[r, ls].astype(BF16), mask_d)
            lhs = jnp.concatenate([kbeta_all[r, ls], qn_all[r, ls]], axis=0).astype(BF16)
            aq = lax.dot_general(lhs, kn_bd, NT_DIMS, preferred_element_type=F32)
            a_cats.append(jnp.where(strict, aq[0:c] * dec, 0.0))
            qk_cats.append(aq[c:2 * c] * dec)
    tinvs = _unit_lower_inverse_cat(a_cats, c, mask_c)
    us, ws = [], []
    for (ch, grp), tinv in zip(units, tinvs):
        r = slice(ch * c, (ch + 1) * c)
        ls = slice(grp * gw, (grp + 1) * gw)
        th, tl = _split(tinv)
        vh_, vl_ = _split(vbeta_all[r, ls])
        kh_, kl_ = _split(kbe_all[r, ls])
        us.append(_dot3s(th, tl, _blockdiag(vh_, mask_d), _blockdiag(vl_, mask_d)))
        ws.append(_dot3s(th, tl, _blockdiag(kh_, mask_d), _blockdiag(kl_, mask_d)))

    for ch in range(tcc // c):
        r = slice(ch * c, (ch + 1) * c)
        glast = g_e[(ch + 1) * c - 1:(ch + 1) * c, :]
        kdec_all = kn_all[r] * jnp.exp(glast - g_e[r])
        eglast = jnp.exp(glast)
        for grp in range(heads // hg):
            un = units.index((ch, grp))
            u, w, qk_cat = us[un], ws[un], qk_cats[un]
            vnew, qs = [], []
            for pr in range(hg // 2):
                idx = grp * (hg // 2) + pr
                la = slice(grp * gw + pr * pw, grp * gw + (pr + 1) * pw)
                lg_ = slice(pr * pw, (pr + 1) * pw)
                s_old = s_ref[idx]
                lhs2 = jnp.concatenate([w[:, lg_], qe_all[r, la]], axis=0).astype(BF16)
                rs = jnp.dot(lhs2, s_old.astype(BF16), preferred_element_type=F32)
                vn = u[:, lg_] - rs[0:c]
                upd = lax.dot_general(kdec_all[:, la].astype(BF16), vn.astype(BF16), TN_DIMS,
                                      preferred_element_type=F32)
                s_ref[idx] = s_old * eglast[:, la] + jnp.where(prow == pcol, upd, 0.0)
                vnew.append(vn)
                qs.append(rs[c:2 * c])
            vn_bd = _blockdiag(jnp.concatenate(vnew, axis=1).astype(BF16), mask_d)
            o_g = jnp.concatenate(qs, axis=1) + jnp.dot(qk_cat.astype(BF16), vn_bd, preferred_element_type=F32)
            if ch * c < tci:
                ro = slice(ch * c, min((ch + 1) * c, tci))
                nr = ro.stop - ro.start
                for a in range(hg):
                    h = grp * hg + a
                    oh = o_g[0:nr, a * B_VAL_DIM:(a + 1) * B_VAL_DIM]
                    zz = z_ref[ro, h * B_VAL_DIM:(h + 1) * B_VAL_DIM]
                    on = oh * lax.rsqrt(jnp.mean(oh * oh, axis=-1, keepdims=True) + RMS_EPS)
                    o_ref[ro, h * B_VAL_DIM:(h + 1) * B_VAL_DIM] = (on * wnorm_ref[...] * _silu(zz)).astype(BF16)

    @pl.when(j == pl.num_programs(1) - 1)
    def _():
        for h in range(heads):
            o = (h % 2) * B_KEY_DIM
            ssmo_ref[h] = s_ref[h // 2, o:o + B_KEY_DIM, o:o + B_VAL_DIM]


def _delta_net(packed, conv0, ssm0, w_conv, a_log, dt_bias, w_onorm, batch, t_pad, t_real, tci, tcc, heads):
    hw = heads * B_KEY_DIM
    nt = t_pad // tci
    assert t_pad % tci == 0 and tcc % DELTA_CHUNK == 0 and tci % SUBLANES == 0
    assert (nt == 1 and tci <= tcc) or tci == tcc
    assert heads % DELTA_GROUP == 0 and DELTA_GROUP % 2 == 0
    wconv_p = jnp.pad(w_conv, ((0, SUBLANES - CONV_WIDTH), (0, 0)))
    alog_row = jnp.zeros((1, LANES), F32).at[0, SM_DECAY:SM_DECAY + heads].set(a_log)
    dtb_row = jnp.zeros((1, LANES), F32).at[0, SM_DECAY:SM_DECAY + heads].set(dt_bias)
    def spread(src, width):
        m = np.zeros((LANES, heads * width), np.float32)
        for h in range(heads):
            m[src + h, h * width:(h + 1) * width] = 1.0
        return jnp.asarray(m, BF16)
    e_beta, e_g, e_gc = spread(SM_BETA, B_KEY_DIM), spread(SM_DECAY, B_KEY_DIM), spread(SM_DECAY, DELTA_CHUNK)
    full2 = lambda a: pl.BlockSpec(a.shape, lambda b, j: (0, 0))
    row = lambda b, j: (b * nt + j)
    return pl.pallas_call(
        functools.partial(_delta_kernel, tci=tci, tcc=tcc, t_real=t_real, heads=heads),
        out_shape=(jax.ShapeDtypeStruct((batch * t_pad, hw), BF16),
                   jax.ShapeDtypeStruct((batch, SUBLANES, 3 * hw), F32),
                   jax.ShapeDtypeStruct((batch, heads, B_KEY_DIM, B_VAL_DIM), F32)),
        grid=(batch, nt),
        in_specs=[pl.BlockSpec((tci, hw), lambda b, j: (row(b, j), QB_OFF // hw)),
                  pl.BlockSpec((tci, hw), lambda b, j: (row(b, j), KB_OFF // hw)),
                  pl.BlockSpec((tci, hw), lambda b, j: (row(b, j), VB_OFF // hw)),
                  pl.BlockSpec((tci, hw), lambda b, j: (row(b, j), Z_OFF // hw)),
                  pl.BlockSpec((tci, LANES), lambda b, j: (row(b, j), SM_OFF // LANES)),
                  pl.BlockSpec((None, SUBLANES, 3 * hw), lambda b, j: (b, 0, 0)),
                  pl.BlockSpec((SUBLANES, 3 * hw), lambda b, j: (0, 0)),
                  pl.BlockSpec((1, LANES), lambda b, j: (0, 0)),
                  pl.BlockSpec((1, LANES), lambda b, j: (0, 0)),
                  pl.BlockSpec((1, B_VAL_DIM), lambda b, j: (0, 0)),
                  pl.BlockSpec((None, heads, B_KEY_DIM, B_VAL_DIM), lambda b, j: (b, 0, 0, 0)),
                  full2(e_beta), full2(e_g), full2(e_gc)],
        out_specs=(pl.BlockSpec((tci, hw), lambda b, j: (row(b, j), 0)),
                   pl.BlockSpec((None, SUBLANES, 3 * hw), lambda b, j: (b, 0, 0)),
                   pl.BlockSpec((None, heads, B_KEY_DIM, B_VAL_DIM), lambda b, j: (b, 0, 0, 0))),
        scratch_shapes=[pltpu.VMEM((tcc + SUBLANES, 3 * hw), F32),
                        pltpu.VMEM((heads // 2, 2 * B_KEY_DIM, 2 * B_VAL_DIM), F32)],
        compiler_params=_cparams("arbitrary", "arbitrary"),
    )(packed, packed, packed, packed, packed, conv0, wconv_p, alog_row, dtb_row,
      w_onorm.reshape(1, B_VAL_DIM), ssm0, e_beta, e_g, e_gc)


def _outproj_kernel(oa_ref, ob_ref, x_ref, wa_ref, wb_ref, gt_ref, sh_ref, sc_ref, g_ref, b_ref,
                    wr_ref, br_ref, x1_ref, u2_ref, lg_ref, *, alpha):
    tm = x_ref.shape[0]
    parts = [slice(p * tm // OUTPROJ_PARTS, (p + 1) * tm // OUTPROJ_PARTS) for p in range(OUTPROJ_PARTS)]

    def rows(ref, r):
        return ref[r, :] if ref.shape[0] == tm else ref[...]

    mixes = [jnp.dot(oa_ref[r, :], wa_ref[...], preferred_element_type=F32)
             + jnp.dot(ob_ref[r, :], wb_ref[...], preferred_element_type=F32) for r in parts]
    for r, mix in zip(parts, mixes):
        x1 = _ln(alpha * x_ref[r, :] + rows(gt_ref, r) * mix) * g_ref[...] + b_ref[...]
        x1_ref[r, :] = x1
        u2 = _ln(x1) * (1.0 + rows(sc_ref, r)) + rows(sh_ref, r)
        u2_ref[r, :] = u2.astype(BF16)
        lg_ref[r, :] = _dot3(u2, wr_ref[...]) + br_ref[...]


def _out_projection(oa, ob, x, wa, wb, gt, sh, sc, ln_g, ln_b, w_r, b_r, rows_per_group, tm, alpha):
    m, d = x.shape
    hwid = oa.shape[1]
    if gt.ndim == 3:
        tpg = rows_per_group // tm
        mod_spec = pl.BlockSpec((None, 1, d), lambda i: (i // tpg, 0, 0))
    else:
        mod_spec = pl.BlockSpec((tm, d), lambda i: (i, 0))
    full = lambda r, c: pl.BlockSpec((r, c), lambda i: (0, 0))
    return pl.pallas_call(
        functools.partial(_outproj_kernel, alpha=alpha),
        out_shape=(jax.ShapeDtypeStruct((m, d), F32), jax.ShapeDtypeStruct((m, d), BF16),
                   jax.ShapeDtypeStruct((m, LANES), F32)),
        grid=(m // tm,),
        in_specs=[pl.BlockSpec((tm, hwid), lambda i: (i, 0)), pl.BlockSpec((tm, hwid), lambda i: (i, 0)),
                  pl.BlockSpec((tm, d), lambda i: (i, 0)), full(hwid, d), full(hwid, d),
                  mod_spec, mod_spec, mod_spec, full(1, d), full(1, d), full(d, LANES), full(1, LANES)],
        out_specs=(pl.BlockSpec((tm, d), lambda i: (i, 0)), pl.BlockSpec((tm, d), lambda i: (i, 0)),
                   pl.BlockSpec((tm, LANES), lambda i: (i, 0))),
        compiler_params=_cparams("arbitrary"),
    )(oa, ob, x, wa, wb, gt, sh, sc, ln_g.reshape(1, d), ln_b.reshape(1, d), w_r, b_r)


def _combine_weights(lg):
    lane = lax.broadcasted_iota(I32, lg.shape, 1)
    big = jnp.int32(LANES)
    gmask = lane < N_GROUPS
    gl = jnp.where(gmask, lg, -jnp.inf)
    gmax = jnp.max(gl, axis=-1, keepdims=True)
    g_idx = jnp.min(jnp.where(gl == gmax, lane, big), axis=-1, keepdims=True)
    p_top = 1.0 / jnp.sum(jnp.where(gmask, jnp.exp(gl - gmax), 0.0), axis=-1, keepdims=True)
    e_lane = lane - RT_EXP
    in_grp = (e_lane >= 0) & (e_lane < N_EXPERTS) & ((e_lane // EXPERTS_PER_GROUP) == g_idx)
    v = jnp.where(in_grp, lg, -jnp.inf)
    v1 = jnp.max(v, axis=-1, keepdims=True)
    i1 = jnp.min(jnp.where(v == v1, lane, big), axis=-1, keepdims=True)
    vr = jnp.where(lane == i1, -jnp.inf, v)
    v2 = jnp.max(vr, axis=-1, keepdims=True)
    i2 = jnp.min(jnp.where(vr == v2, lane, big), axis=-1, keepdims=True)
    e2 = jnp.exp(v2 - v1)
    den = 1.0 / (1.0 + e2)
    return jnp.where(lane == i1, den * p_top, jnp.where(lane == i2, e2 * den * p_top, 0.0))


def _moe_kernel(u_ref, lg_ref, x1_ref, wg_ref, wu_ref, wd_ref, gt_ref, g_ref, b_ref, o_ref,
                acc_ref, cmb_ref, *, alpha):
    e = pl.program_id(1)

    @pl.when(e == 0)
    def _():
        cmb_ref[...] = _combine_weights(lg_ref[...])
        acc_ref[...] = jnp.zeros(acc_ref.shape, F32)

    u = u_ref[...]
    h = jnp.dot(u, wg_ref[...], preferred_element_type=F32)
    up = jnp.dot(u, wu_ref[...], preferred_element_type=F32)
    lane = lax.broadcasted_iota(I32, cmb_ref.shape, 1)
    c_e = jnp.sum(jnp.where(lane == e + RT_EXP, cmb_ref[...], 0.0), axis=-1, keepdims=True)
    act = _silu(h) * up * c_e
    acc_ref[...] += jnp.dot(act.astype(BF16), wd_ref[...], preferred_element_type=F32)

    @pl.when(e == pl.num_programs(1) - 1)
    def _():
        o_ref[...] = _ln(alpha * x1_ref[...] + gt_ref[...] * acc_ref[...]) * g_ref[...] + b_ref[...]


def _moe(u2, logits, x1, wg, wu, wd, gt, ln_g, ln_b, rows_per_group, tm, alpha):
    m, d = x1.shape
    ne, _, de = wg.shape
    if gt.ndim == 3:
        tpg = rows_per_group // tm
        mod_spec = pl.BlockSpec((None, 1, d), lambda i, e: (i // tpg, 0, 0))
    else:
        mod_spec = pl.BlockSpec((tm, d), lambda i, e: (i, 0))
    return pl.pallas_call(
        functools.partial(_moe_kernel, alpha=alpha),
        out_shape=jax.ShapeDtypeStruct((m, d), F32),
        grid=(m // tm, ne),
        in_specs=[pl.BlockSpec((tm, d), lambda i, e: (i, 0)),
                  pl.BlockSpec((tm, LANES), lambda i, e: (i, 0)),
                  pl.BlockSpec((tm, d), lambda i, e: (i, 0)),
                  pl.BlockSpec((None, d, de), lambda i, e: (e, 0, 0)),
                  pl.BlockSpec((None, d, de), lambda i, e: (e, 0, 0)),
                  pl.BlockSpec((None, de, d), lambda i, e: (e, 0, 0)),
                  mod_spec,
                  pl.BlockSpec((1, d), lambda i, e: (0, 0)),
                  pl.BlockSpec((1, d), lambda i, e: (0, 0))],
        out_specs=pl.BlockSpec((tm, d), lambda i, e: (i, 0)),
        scratch_shapes=[pltpu.VMEM((tm, d), F32), pltpu.VMEM((tm, LANES), F32)],
        compiler_params=_cparams("arbitrary", "arbitrary"),
    )(u2, logits, x1, wg, wu, wd, gt, ln_g.reshape(1, d), ln_b.reshape(1, d))


def _layer(x2d, mod, groups, rows_per_group, t_real, pos, w, attend, conv0, ssm0, tm, delta_tiles):
    m, d = x2d.shape
    heads = d // (2 * HEAD_DIM)
    alpha = float((2 * 1) ** 0.25)
    sh1, sc1, gt1, sh2, sc2, gt2 = jnp.split(mod, 6, axis=-1)
    if rows_per_group % tm == 0:
        expand = lambda a: a[:, None, :]
    else:
        expand = lambda a: jnp.repeat(a, rows_per_group, axis=0)
    sh1, sc1, gt1, sh2, sc2, gt2 = [expand(a) for a in (sh1, sc1, gt1, sh2, sc2, gt2)]
    c128, s128 = _rope_tables(pos, HEAD_DIM // 2)
    c64, s64 = _rope_tables(pos, IDX_DIM // 2)
    tabs = (c128, s128, c64, s64)
    if rows_per_group % tm != 0:
        tabs = tuple(jnp.tile(t, (tm // rows_per_group, 1)) for t in tabs)
    packed = _in_projection(x2d, sh1, sc1, w["w_in"], tabs, rows_per_group, tm)
    o_a = attend(packed)
    tci, tcc = delta_tiles
    o_b, conv_new, ssm_new = _delta_net(packed, conv0, ssm0, w["w_conv"], w["a_log"], w["dt_bias"],
                                        w["w_onorm"], groups, rows_per_group, t_real, tci, tcc, heads)
    x1, u2, logits = _out_projection(o_a, o_b, x2d, w["w_out_a"], w["w_out_b"], gt1, sh2, sc2,
                                     w["ln1_g"], w["ln1_b"], w["w_router"], w["b_router"],
                                     rows_per_group, min(tm, OUTPROJ_TM), alpha)
    x2 = _moe(u2, logits, x1, w["w_gate"], w["w_up"], w["w_down"], gt2, w["ln2_g"], w["ln2_b"],
              rows_per_group, min(tm, MOE_TM), alpha)
    return x2, packed, conv_new, ssm_new


def kernel(x_prompt, x_sample, cache_k, cache_v, cache_kidx, state_conv, state_ssm, page_table,
           c_prompt, c_sample, w_mod, b_mod, w_in, w_conv, a_log, dt_bias, w_onorm, w_out,
           ln1_g, ln1_b, w_grp, b_grp, w_erouter, b_erouter, w_gate, w_up, w_down, ln2_g, ln2_b):
    assert w_mod.shape[0] == 1, "single layer"
    batch, seq, d = x_prompt.shape
    nb, dec_seq, _ = x_sample.shape
    heads = d // (2 * HEAD_DIM)
    hw = heads * B_KEY_DIM
    past_len = page_table.shape[1] * cache_k.shape[2]
    kvw = A_KV_HEADS * HEAD_DIM

    n_rt = N_GROUPS + N_EXPERTS
    w_router = jnp.pad(jnp.concatenate([w_grp[0], w_erouter[0]], axis=1), ((0, 0), (0, LANES - n_rt)))
    b_router = jnp.pad(jnp.concatenate([b_grp[0], b_erouter[0]]), (0, LANES - n_rt)).reshape(1, LANES)
    w = dict(w_in=_pack_w_in(w_in[0], d), w_conv=w_conv[0], a_log=a_log[0], dt_bias=dt_bias[0],
             w_onorm=w_onorm[0], w_out_a=w_out[0, :hw].astype(BF16), w_out_b=w_out[0, hw:].astype(BF16),
             ln1_g=ln1_g[0], ln1_b=ln1_b[0], w_router=w_router, b_router=b_router,
             w_gate=w_gate[0].astype(BF16), w_up=w_up[0].astype(BF16), w_down=w_down[0].astype(BF16),
             ln2_g=ln2_g[0], ln2_b=ln2_b[0])

    n_c = batch + nb
    c_all = jnp.pad(jnp.concatenate([c_prompt, c_sample], axis=0), ((0, (-n_c) % SUBLANES), (0, 0)))
    mod = _modulation(c_all, w_mod[0], b_mod[0])

    def attend_prompt(packed):
        k_bf = packed[:, KA_OFF:KA_OFF + kvw].astype(BF16)
        v_bf = packed[:, VA_OFF:VA_OFF + kvw].astype(BF16)
        ki_bf = packed[:, SM_OFF + SM_KI:SM_OFF + SM_KI + IDX_DIM].astype(BF16)
        return _dsa_prompt(packed, k_bf, v_bf, ki_bf, batch, seq, heads)

    tm_p = min(INPROJ_TM, seq)
    conv0_p = jnp.zeros((batch, SUBLANES, 3 * hw), F32)
    ssm0_p = jnp.zeros((batch, heads, B_KEY_DIM, B_VAL_DIM), F32)
    tc_p = min(DELTA_TM, seq)
    y_p, packed_p, conv_p, ssm_p = _layer(
        x_prompt.reshape(batch * seq, d), mod[:batch], batch, seq, seq, jnp.arange(seq, dtype=I32), w,
        attend_prompt, conv0_p, ssm0_p, tm_p, (tc_p, tc_p))

    rows = SUBLANES
    assert CONV_WIDTH - 1 <= dec_seq <= rows
    x_s = jnp.pad(x_sample, ((0, 0), (0, rows - dec_seq), (0, 0))).reshape(nb * rows, d)
    pos_s = past_len + jnp.arange(rows, dtype=I32)

    def attend_sample(packed):
        o = _dsa_sample(packed.reshape(nb, rows, PACK_W), cache_k, cache_v, cache_kidx,
                        page_table, dec_seq, heads)
        return o.reshape(nb * rows, heads * HEAD_DIM)

    conv0_s = jnp.pad(state_conv[0], ((0, 0), (rows - (CONV_WIDTH - 1), 0), (0, 0)))
    tm_s = min(256, nb * rows)
    y_s, packed_s, conv_s, ssm_s = _layer(
        x_s, mod[batch:batch + nb], nb, rows, dec_seq, pos_s, w, attend_sample, conv0_s, state_ssm[0],
        tm_s, (rows, DELTA_CHUNK))

    def states(packed, groups, t_pad, t):
        p = packed.reshape(groups, t_pad, PACK_W)[:, :t]
        k = p[..., KA_OFF:KA_OFF + kvw].reshape(1, groups, t, A_KV_HEADS, HEAD_DIM)
        v = p[..., VA_OFF:VA_OFF + kvw].reshape(1, groups, t, A_KV_HEADS, HEAD_DIM)
        ki = p[..., SM_OFF + SM_KI:SM_OFF + SM_KI + IDX_DIM][None]
        return k, v, ki

    k_p, v_p, ki_p = states(packed_p, batch, seq, seq)
    k_s, v_s, ki_s = states(packed_s, nb, rows, dec_seq)
    conv_p = conv_p[None, :, rows - (CONV_WIDTH - 1):]
    conv_s = conv_s[None, :, rows - (CONV_WIDTH - 1):]
    y_s = y_s.reshape(nb, rows, d)[:, :dec_seq]
    return (y_p.reshape(batch, seq, d), y_s, k_p, v_p, ki_p, conv_p, ssm_p[None],
            k_s, v_s, ki_s, conv_s, ssm_s[None])
```
